```python
import math
import jax
import jax.numpy as jnp
from jax import lax
import numpy as np

D_MODEL = 1024
BATCH = 4
SEQ = 4096
DEPTH = 2

HEAD_DIM = 64
D_MIX = D_MODEL
N_HEADS_MIX = D_MIX // HEAD_DIM
A_HEADS = N_HEADS_MIX // 4
A_WIDTH = A_HEADS * HEAD_DIM
DILATED_PATTERNS = ((128, 1), (512, 4), (2048, 16))
C_HEADS = N_HEADS_MIX // 4
C_WIDTH = C_HEADS * HEAD_DIM
MOBA_BLOCK = 256
MOBA_TOPK = 3
MOBA_Q_CHUNK = 64
B_HEADS = N_HEADS_MIX - A_HEADS - C_HEADS
SSM_HEAD_DIM = HEAD_DIM
D_SSM = B_HEADS * SSM_HEAD_DIM
SSM_STATE = 128
SSM_GROUPS = 2
SSM_CONV = 4
SSM_CHUNK = 128
CONV_DIM = D_SSM + 2 * SSM_GROUPS * SSM_STATE
D_FF = 4 * D_MODEL
D_IN = 3 * A_WIDTH + 3 * C_WIDTH + D_SSM + CONV_DIM + B_HEADS
H_ATTN = A_HEADS + C_HEADS
EPS = 1e-6
NEG_INF = -1e30

kernel_name = "hybrid_dilated_ssd_moba_block"


def rms_norm(x, w):
    xf = x.astype(jnp.float32)
    y = xf * lax.rsqrt(jnp.mean(xf * xf, axis=-1, keepdims=True) + EPS)
    return (y * w.astype(jnp.float32)).astype(x.dtype)


def alibi_slopes():
    return 2.0 ** (-8.0 * jnp.arange(1, H_ATTN + 1, dtype=jnp.float32) / H_ATTN)


def dilated_band_attention(q, k, v, slopes, window, dilation):
    bsz, seq, nh, hd = q.shape
    n_off = window // dilation
    blk = n_off
    length = seq // dilation
    n_blk = -(-length // blk)
    lp = n_blk * blk

    def to_sub(t):
        return t.reshape(bsz, length, dilation, nh, hd).transpose(0, 2, 1, 3, 4)

    qs = jnp.pad(to_sub(q), ((0, 0), (0, 0), (0, lp - length), (0, 0), (0, 0)))
    qs = qs.reshape(bsz, dilation, n_blk, blk, nh, hd)
    pad_kv = ((0, 0), (0, 0), (blk, lp - length), (0, 0), (0, 0))

    def windows(t):
        tb = jnp.pad(to_sub(t), pad_kv).reshape(bsz, dilation, n_blk + 1, blk, nh, hd)
        return jnp.concatenate([tb[:, :, :-1], tb[:, :, 1:]], axis=3)

    kw, vw = windows(k), windows(v)
    logits = jnp.einsum('brmahe,brmche->brhmac', qs, kw) * (hd ** -0.5)
    a_idx = jnp.arange(blk)[:, None]
    c_idx = jnp.arange(2 * blk)[None, :]
    offset = a_idx + blk - c_idx
    key_idx = jnp.arange(n_blk)[:, None, None] * blk - blk + c_idx[None]
    valid = (offset >= 0) & (offset <= n_off) & (key_idx >= 0)
    bias = -slopes[:, None, None, None] * (dilation * offset).astype(jnp.float32)
    logits = jnp.where(valid, logits + bias, NEG_INF)
    lse = jax.nn.logsumexp(logits, axis=-1)
    probs = jnp.exp(logits - lse[..., None])
    out = jnp.einsum('brhmac,brmche->brmahe', probs, vw)
    out = out.reshape(bsz, dilation, lp, nh, hd)[:, :, :length]
    out = out.transpose(0, 2, 1, 3, 4).reshape(bsz, seq, nh, hd)
    lse = lse.transpose(0, 1, 3, 4, 2).reshape(bsz, dilation, lp, nh)[:, :, :length]
    lse = lse.transpose(0, 2, 1, 3).reshape(bsz, seq, nh)
    return out, lse


def dilated_mixture(q, k, v, slopes):
    outs, lses = [], []
    for window, dilation in DILATED_PATTERNS:
        o, l = dilated_band_attention(q, k, v, slopes, window, dilation)
        outs.append(o)
        lses.append(l)
    w = jax.nn.softmax(jnp.stack(lses, axis=0), axis=0)
    return jnp.einsum('pbsh,pbshe->bshe', w, jnp.stack(outs, axis=0))


def causal_depthwise_conv(x, w, b):
    out = lax.conv_general_dilated(
        x, w[:, None, :].astype(x.dtype), window_strides=(1,),
        padding=((SSM_CONV - 1, 0),), dimension_numbers=('NWC', 'WIO', 'NWC'),
        feature_group_count=x.shape[-1])
    return out + b.astype(x.dtype)


def segsum_exp(a_cs):
    l = a_cs.shape[-1]
    mask = jnp.tril(jnp.ones((l, l), dtype=bool))
    diff = a_cs[..., :, None] - a_cs[..., None, :]
    return jnp.where(mask, jnp.exp(jnp.where(mask, diff, 0.0)), 0.0)


def ssd_scan(x, dt, a, b_mat, c_mat):
    bsz, seq, nh, hp = x.shape
    nc, l = seq // SSM_CHUNK, SSM_CHUNK
    xd = (x * dt[..., None]).reshape(bsz, nc, l, nh, hp)
    da = (dt * a).reshape(bsz, nc, l, nh).transpose(0, 1, 3, 2)
    bc = b_mat.reshape(bsz, nc, l, nh, -1)
    cc = c_mat.reshape(bsz, nc, l, nh, -1)
    a_cs = jnp.cumsum(da, axis=-1)
    scores = jnp.einsum('bclhn,bcshn->bchls', cc, bc) * segsum_exp(a_cs)
    y_diag = jnp.einsum('bchls,bcshp->bclhp', scores, xd)
    decay_states = jnp.exp(a_cs[..., -1:] - a_cs)
    chunk_states = jnp.einsum('bclhn,bchl,bclhp->bchpn', bc, decay_states, xd)
    chunk_decay = jnp.exp(a_cs[..., -1])

    def step(state, inp):
        cs, dec = inp
        return state * dec[..., None, None] + cs, state

    init = jnp.zeros((bsz, nh, hp, bc.shape[-1]), jnp.float32)
    _, states_in = lax.scan(step, init, (chunk_states.transpose(1, 0, 2, 3, 4),
                                         chunk_decay.transpose(1, 0, 2)))
    states_in = states_in.transpose(1, 0, 2, 3, 4)
    y_off = jnp.einsum('bclhn,bchpn,bchl->bclhp', cc, states_in, jnp.exp(a_cs))
    return (y_diag + y_off).reshape(bsz, seq, nh, hp)


def mamba2_mixer(z, xbc, dt_raw, conv_w, conv_b, dt_bias, a_log, d_skip, norm_w):
    bsz, seq, _ = z.shape
    xbc = jax.nn.silu(causal_depthwise_conv(xbc, conv_w, conv_b)).astype(jnp.float32)
    xs = xbc[..., :D_SSM].reshape(bsz, seq, B_HEADS, SSM_HEAD_DIM)
    rep = B_HEADS // SSM_GROUPS
    bm = jnp.repeat(xbc[..., D_SSM:D_SSM + SSM_GROUPS * SSM_STATE].reshape(bsz, seq, SSM_GROUPS, SSM_STATE), rep, axis=2)
    cm = jnp.repeat(xbc[..., D_SSM + SSM_GROUPS * SSM_STATE:].reshape(bsz, seq, SSM_GROUPS, SSM_STATE), rep, axis=2)
    dt = jax.nn.softplus(dt_raw.astype(jnp.float32) + dt_bias.astype(jnp.float32))
    a = -jnp.exp(a_log.astype(jnp.float32))
    y = ssd_scan(xs, dt, a, bm, cm) + d_skip.astype(jnp.float32)[:, None] * xs
    y = y.reshape(bsz, seq, D_SSM) * jax.nn.silu(z.astype(jnp.float32))
    y = rms_norm(y.reshape(bsz, seq, SSM_GROUPS, D_SSM // SSM_GROUPS),
                 norm_w.reshape(SSM_GROUPS, D_SSM // SSM_GROUPS))
    return y.reshape(bsz, seq, D_SSM)


def moba_attention(q, k, v, slopes):
    bsz, seq, nh, hd = q.shape
    n_blk = -(-seq // MOBA_BLOCK)
    sp = n_blk * MOBA_BLOCK
    pad = ((0, 0), (0, sp - seq), (0, 0), (0, 0))
    q, k, v = jnp.pad(q, pad), jnp.pad(k, pad), jnp.pad(v, pad)
    kb = k.reshape(bsz, n_blk, MOBA_BLOCK, nh, hd)
    vb = v.reshape(bsz, n_blk, MOBA_BLOCK, nh, hd)
    k_mean = kb.mean(axis=2)
    kbh = kb.transpose(0, 3, 1, 2, 4)
    vbh = vb.transpose(0, 3, 1, 2, 4)
    n_sel = min(MOBA_TOPK, n_blk - 1)
    chunks_per_blk = MOBA_BLOCK // MOBA_Q_CHUNK
    n_chunks = sp // MOBA_Q_CHUNK
    scale = hd ** -0.5
    q_chunks = q.reshape(bsz, n_chunks, MOBA_Q_CHUNK, nh, hd).transpose(1, 0, 2, 3, 4)
    in_blk = jnp.arange(MOBA_BLOCK)
    b_idx = jnp.arange(bsz)[:, None, None, None]
    h_idx = jnp.arange(nh)[None, None, :, None]

    def chunk_fn(args):
        qi, ci = args
        blk = ci // chunks_per_blk
        pos_q = ci * MOBA_Q_CHUNK + jnp.arange(MOBA_Q_CHUNK)
        k_own = lax.dynamic_index_in_dim(kb, blk, axis=1, keepdims=False)
        v_own = lax.dynamic_index_in_dim(vb, blk, axis=1, keepdims=False)
        dist_own = pos_q[:, None] - (blk * MOBA_BLOCK + in_blk)[None, :]
        logit_own = (jnp.einsum('bqhe,bkhe->bhqk', qi, k_own) * scale
                     - slopes[:, None, None] * dist_own.astype(jnp.float32))
        logit_own = jnp.where(dist_own >= 0, logit_own, NEG_INF)
        if n_sel == 0:
            p_own = jax.nn.softmax(logit_own, axis=-1)
            return jnp.einsum('bhqk,bkhe->bqhe', p_own, v_own)
        gate = jnp.einsum('bqhe,bnhe->bqhn', qi, k_mean)
        gate = jnp.where(jnp.arange(n_blk) < blk, gate, NEG_INF)
        _, idx = lax.top_k(gate, n_sel)
        k_sel = kbh[b_idx, h_idx, idx]
        v_sel = vbh[b_idx, h_idx, idx]
        dist_sel = pos_q[None, :, None, None, None] - (idx[..., None] * MOBA_BLOCK + in_blk)
        logit_sel = (jnp.einsum('bqhe,bqhsje->bqhsj', qi, k_sel) * scale
                     - slopes[None, None, :, None, None] * dist_sel.astype(jnp.float32))
        sel_valid = jnp.arange(n_sel) < blk
        logit_sel = jnp.where(sel_valid[:, None], logit_sel, NEG_INF)
        logit_sel = logit_sel.reshape(bsz, MOBA_Q_CHUNK, nh, n_sel * MOBA_BLOCK).transpose(0, 2, 1, 3)
        probs = jax.nn.softmax(jnp.concatenate([logit_own, logit_sel], axis=-1), axis=-1)
        p_own = probs[..., :MOBA_BLOCK]
        p_sel = probs[..., MOBA_BLOCK:].transpose(0, 2, 1, 3).reshape(bsz, MOBA_Q_CHUNK, nh, n_sel, MOBA_BLOCK)
        return (jnp.einsum('bhqk,bkhe->bqhe', p_own, v_own)
                + jnp.einsum('bqhsj,bqhsje->bqhe', p_sel, v_sel))

    out = lax.map(chunk_fn, (q_chunks, jnp.arange(n_chunks)))
    return out.transpose(1, 0, 2, 3, 4).reshape(bsz, sp, nh, hd)[:, :seq]


def hybrid_layer(x, norm1_w, w_in, a_q_norm, a_k_norm, c_q_norm, c_k_norm,
                 conv_w, conv_b, dt_bias, a_log, d_skip, ssm_norm_w, w_out,
                 norm2_w, w_mlp_in, w_mlp_out, slopes):
    bsz, seq, _ = x.shape
    h = rms_norm(x, norm1_w)
    proj = h @ w_in
    sizes = [A_WIDTH] * 3 + [C_WIDTH] * 3 + [D_SSM, CONV_DIM]
    points, acc = [], 0
    for s in sizes:
        acc += s
        points.append(acc)
    qa, ka, va, qc, kc, vc, z, xbc, dt_raw = jnp.split(proj, points, axis=-1)

    def heads(t, n):
        return t.reshape(bsz, seq, n, HEAD_DIM)

    f32 = jnp.float32
    ya = dilated_mixture(rms_norm(heads(qa, A_HEADS), a_q_norm).astype(f32),
                         rms_norm(heads(ka, A_HEADS), a_k_norm).astype(f32),
                         heads(va, A_HEADS).astype(f32), slopes[0::2])
    yb = mamba2_mixer(z, xbc, dt_raw, conv_w, conv_b, dt_bias, a_log, d_skip, ssm_norm_w)
    yc = moba_attention(rms_norm(heads(qc, C_HEADS), c_q_norm).astype(f32),
                        rms_norm(heads(kc, C_HEADS), c_k_norm).astype(f32),
                        heads(vc, C_HEADS).astype(f32), slopes[1::2])
    y = jnp.concatenate([ya.reshape(bsz, seq, A_WIDTH), yb, yc.reshape(bsz, seq, C_WIDTH)],
                        axis=-1).astype(x.dtype)
    x = x + y @ w_out
    h = rms_norm(x, norm2_w)
    return x + jnp.square(jax.nn.relu(h @ w_mlp_in)) @ w_mlp_out


def setup_inputs(seed: int = 0) -> dict:
    key = jax.random.key(seed)
    ks = jax.random.split(key, 17)
    f32 = jnp.float32

    def nrm(k, shape, scale):
        return jax.random.normal(k, shape, f32) * scale

    def gain(k, shape, noise=0.02):
        return 1.0 + noise * jax.random.normal(k, shape, f32)

    dt = jnp.exp(jax.random.uniform(ks[9], (DEPTH, B_HEADS), f32, math.log(1e-3), math.log(1e-1)))
    return {
        "x": jax.random.normal(ks[0], (BATCH, SEQ, D_MODEL), f32),
        "norm1_w": gain(ks[1], (DEPTH, D_MODEL)),
        "w_in": nrm(ks[2], (DEPTH, D_MODEL, D_IN), D_MODEL ** -0.5),
        "a_q_norm": gain(ks[3], (DEPTH, HEAD_DIM)),
        "a_k_norm": gain(ks[4], (DEPTH, HEAD_DIM)),
        "c_q_norm": gain(ks[5], (DEPTH, HEAD_DIM)),
        "c_k_norm": gain(ks[6], (DEPTH, HEAD_DIM)),
        "conv_w": nrm(ks[7], (DEPTH, SSM_CONV, CONV_DIM), SSM_CONV ** -0.5),
        "conv_b": nrm(ks[8], (DEPTH, CONV_DIM), 0.02),
        "dt_bias": dt + jnp.log(-jnp.expm1(-dt)),
        "a_log": jnp.log(jax.random.uniform(ks[10], (DEPTH, B_HEADS), f32, 1.0, 16.0)),
        "d_skip": gain(ks[11], (DEPTH, B_HEADS), 0.1),
        "ssm_norm_w": gain(ks[12], (DEPTH, D_SSM)),
        "w_out": nrm(ks[13], (DEPTH, D_MIX, D_MODEL), D_MIX ** -0.5),
        "norm2_w": gain(ks[14], (DEPTH, D_MODEL)),
        "w_mlp_in": nrm(ks[15], (DEPTH, D_MODEL, D_FF), D_MODEL ** -0.5),
        "w_mlp_out": nrm(ks[16], (DEPTH, D_FF, D_MODEL), D_FF ** -0.5),
    }


def reference(x, norm1_w, w_in, a_q_norm, a_k_norm, c_q_norm, c_k_norm, conv_w, conv_b,
              dt_bias, a_log, d_skip, ssm_norm_w, w_out, norm2_w, w_mlp_in, w_mlp_out):
    slopes = alibi_slopes()
    for i in range(DEPTH):
        x = hybrid_layer(x, norm1_w[i], w_in[i], a_q_norm[i], a_k_norm[i], c_q_norm[i],
                         c_k_norm[i], conv_w[i], conv_b[i], dt_bias[i], a_log[i], d_skip[i],
                         ssm_norm_w[i], w_out[i], norm2_w[i], w_mlp_in[i], w_mlp_out[i], slopes)
    return x
```

```python
import functools

import jax
import jax.numpy as jnp
from jax import lax
from jax.experimental import pallas as pl
from jax.experimental.pallas import tpu as pltpu

F32 = jnp.float32
BF16 = jnp.bfloat16

D_MODEL = 1024
HEAD_DIM = 64
N_ATTN_HEADS = 4
ATTN_WIDTH = N_ATTN_HEADS * HEAD_DIM
DILATED_PATTERNS = ((128, 1), (512, 4), (2048, 16))
BAND = 128
MOBA_BLOCK = 256
MOBA_TOPK = 3
SSM_HEADS = 8
D_SSM = SSM_HEADS * HEAD_DIM
SSM_STATE = 128
SSM_GROUPS = 2
SSM_CONV = 4
SSM_CHUNK = 128
CONV_DIM = D_SSM + 2 * SSM_GROUPS * SSM_STATE
D_FF = 4 * D_MODEL
D_IN = 2 * 3 * ATTN_WIDTH + D_SSM + CONV_DIM + SSM_HEADS
LANES = 128
DT_PAD = LANES
D_IN_PAD = D_IN - SSM_HEADS + DT_PAD
EPS = 1e-6
NEG_INF = -1e30
SLOPES_A = tuple(2.0 ** -(2 * h + 1) for h in range(N_ATTN_HEADS))
SLOPES_C = tuple(2.0 ** -(2 * h + 2) for h in range(N_ATTN_HEADS))

VMEM_LIMIT = 56 * 1024 * 1024


def _split3(a):
    hi = a.astype(BF16)
    r1 = a - hi.astype(F32)
    mid = r1.astype(BF16)
    lo = (r1 - mid.astype(F32)).astype(BF16)
    return hi, mid, lo


def _dot(a, b):
    return jnp.dot(a, b, preferred_element_type=F32)


def _dot_nt(a, b):
    return lax.dot_general(a, b, (((1,), (1,)), ((), ())), preferred_element_type=F32)


def _dot_tn(a, b):
    return lax.dot_general(a, b, (((0,), (0,)), ((), ())), preferred_element_type=F32)


def _dot_f32_lhs(a, sel):
    hi, mid, lo = _split3(a)
    return _dot(hi, sel) + _dot(mid, sel) + _dot(lo, sel)


def _dot_f32_rhs(sel, b):
    hi, mid, lo = _split3(b)
    return _dot(sel, hi) + _dot(sel, mid) + _dot(sel, lo)


def _head_masks(width, dtype):
    lane = lax.broadcasted_iota(jnp.int32, (1, width), 1)
    return [(lane // HEAD_DIM == h).astype(dtype) for h in range(width // HEAD_DIM)]


def _row_slopes(rows_per_head, slopes):
    row = lax.broadcasted_iota(jnp.int32, (len(slopes) * rows_per_head, 1), 0) // rows_per_head
    out = jnp.full(row.shape, slopes[-1], F32)
    for h in range(len(slopes) - 2, -1, -1):
        out = jnp.where(row == h, slopes[h], out)
    return out


def _in_proj_kernel(x_ref, n1_ref, w_ref, g_ref, bd_ref,
                    qkva_ref, qkvc_ref, z_ref, xbc_ref, dt_ref):
    x = x_ref[...]
    ms = jnp.mean(x * x, axis=-1, keepdims=True)
    h = (x * lax.rsqrt(ms + EPS) * n1_ref[...]).astype(BF16)
    bd = bd_ref[...]

    def seg(lo, width):
        return _dot(h, w_ref[:, lo:lo + width])

    def head_norm(y, g):
        y2 = y * y
        hi = y2.astype(BF16)
        lo = (y2 - hi.astype(F32)).astype(BF16)
        ss = _dot(hi, bd) + _dot(lo, bd)
        return y * lax.rsqrt(ss * (1.0 / HEAD_DIM) + EPS) * g

    w = ATTN_WIDTH
    for i, out_ref in enumerate((qkva_ref, qkvc_ref)):
        base = 3 * w * i
        out_ref[:, 0:w] = head_norm(seg(base, w), g_ref[2 * i:2 * i + 1, :]).astype(BF16)
        out_ref[:, w:2 * w] = head_norm(seg(base + w, w), g_ref[2 * i + 1:2 * i + 2, :]).astype(BF16)
        out_ref[:, 2 * w:3 * w] = seg(base + 2 * w, w).astype(BF16)
    z_ref[...] = seg(6 * w, D_SSM)
    xbc_ref[...] = seg(6 * w + D_SSM, CONV_DIM)
    dt_ref[...] = seg(6 * w + D_SSM + CONV_DIM, DT_PAD)


def _in_proj(x2d, n1, w_pad, gains, bd, *, tm=512):
    t = x2d.shape[0]
    row = lambda i: (i, 0)
    const = lambda i: (0, 0)
    return pl.pallas_call(
        _in_proj_kernel,
        grid=(t // tm,),
        in_specs=[
            pl.BlockSpec((tm, D_MODEL), row),
            pl.BlockSpec((1, D_MODEL), const),
            pl.BlockSpec((D_MODEL, D_IN_PAD), const),
            pl.BlockSpec((4, ATTN_WIDTH), const),
            pl.BlockSpec((ATTN_WIDTH, ATTN_WIDTH), const),
        ],
        out_specs=[
            pl.BlockSpec((tm, 3 * ATTN_WIDTH), row),
            pl.BlockSpec((tm, 3 * ATTN_WIDTH), row),
            pl.BlockSpec((tm, D_SSM), row),
            pl.BlockSpec((tm, CONV_DIM), row),
            pl.BlockSpec((tm, DT_PAD), row),
        ],
        out_shape=[
            jax.ShapeDtypeStruct((t, 3 * ATTN_WIDTH), BF16),
            jax.ShapeDtypeStruct((t, 3 * ATTN_WIDTH), BF16),
            jax.ShapeDtypeStruct((t, D_SSM), F32),
            jax.ShapeDtypeStruct((t, CONV_DIM), F32),
            jax.ShapeDtypeStruct((t, DT_PAD), F32),
        ],
        compiler_params=pltpu.CompilerParams(
            dimension_semantics=("parallel",), vmem_limit_bytes=VMEM_LIMIT),
        name="in_proj",
    )(x2d, n1, w_pad, gains, bd)


def _band_attn_kernel(q_ref, k_ref, v_ref, o_ref, lse_ref, *, dil, length):
    nh = N_ATTN_HEADS
    rows = nh * BAND
    hm_b = _head_masks(ATTN_WIDTH, BF16)
    hm_f = _head_masks(ATTN_WIDTH, F32)
    slope = _row_slopes(BAND, SLOPES_A)
    a = lax.broadcasted_iota(jnp.int32, (rows, 2 * BAND), 0) % BAND
    c = lax.broadcasted_iota(jnp.int32, (rows, 2 * BAND), 1)

    def bias_for(offset):
        valid = (offset >= 0) & (offset <= BAND)
        return jnp.where(valid, -(slope * float(dil)) * offset.astype(F32), NEG_INF)

    bias_first = bias_for(a - c)
    bias_rest = bias_for(a + BAND - c)

    def tile(q0, k0, bias):
        q = q_ref[0, pl.ds(q0, BAND), :]
        q4 = jnp.concatenate([q * hm_b[h] for h in range(nh)], axis=0)
        kw = k_ref[0, pl.ds(k0, 2 * BAND), :]
        vw = v_ref[0, pl.ds(k0, 2 * BAND), :]
        s = _dot_nt(q4, kw) + bias
        mx = jnp.max(s, axis=-1, keepdims=True)
        p = jnp.exp(s - mx)
        l = jnp.sum(p, axis=-1, keepdims=True)
        o4 = _dot(p.astype(BF16), vw)
        inv = 1.0 / l
        lse = mx + jnp.log(l)
        out = jnp.zeros((BAND, ATTN_WIDTH), F32)
        lse_full = jnp.zeros((BAND, ATTN_WIDTH), F32)
        for h in range(nh):
            sl = slice(h * BAND, (h + 1) * BAND)
            out = out + (o4[sl] * inv[sl]) * hm_f[h]
            lse_full = lse_full + lse[sl] * hm_f[h]
        o_ref[0, pl.ds(q0, BAND), :] = out
        lse_ref[0, pl.ds(q0, BAND), :] = lse_full

    tile(0, 0, bias_first)

    def body(m, carry):
        q0 = pl.multiple_of(m * BAND, BAND)
        tile(q0, pl.multiple_of(q0 - BAND, BAND), bias_rest)
        return carry

    lax.fori_loop(1, length // BAND, body, 0)


def _band_attn(qkv, dil):
    b, s, _ = qkv.shape
    length = s // dil
    w = ATTN_WIDTH
    view = qkv.reshape(b, length, dil * 3 * w)

    def in_spec(j):
        return pl.BlockSpec((1, length, w), lambda bi, r: (bi, 0, 3 * r + j))

    out_spec = pl.BlockSpec((1, length, w), lambda bi, r: (bi, 0, r))
    out, lse = pl.pallas_call(
        functools.partial(_band_attn_kernel, dil=dil, length=length),
        grid=(b, dil),
        in_specs=[in_spec(0), in_spec(1), in_spec(2)],
        out_specs=[out_spec, out_spec],
        out_shape=[jax.ShapeDtypeStruct((b, length, dil * w), F32)] * 2,
        compiler_params=pltpu.CompilerParams(
            dimension_semantics=("parallel", "parallel"), vmem_limit_bytes=VMEM_LIMIT),
        name=f"band_attn_d{dil}",
    )(view, view, view)
    return out.reshape(b, s, w), lse.reshape(b, s, w)


def _softplus(x):
    return jnp.maximum(x, 0.0) + jnp.log1p(jnp.exp(-jnp.abs(x)))


def _silu(x):
    return x / (1.0 + jnp.exp(-x))


def _ssd_kernel(xbc_ref, z_ref, dt_ref, cw_ref, cb_ref, dtb_ref, alog_ref, dsk_ref, nw_ref,
                y_ref, ext_ref, tail_ref, state_ref, *, chunks):
    rows = chunks * SSM_CHUNK
    halo = 8
    cl = SSM_CHUNK
    gw = D_SSM // SSM_GROUPS
    hpg = SSM_HEADS // SSM_GROUPS

    @pl.when(pl.program_id(1) == 0)
    def _():
        tail_ref[...] = jnp.zeros_like(tail_ref)
        state_ref[...] = jnp.zeros_like(state_ref)

    ext_ref[0:halo, :] = tail_ref[...]
    ext_ref[halo:halo + rows, :] = xbc_ref[0]
    tail_ref[...] = xbc_ref[0, rows - halo:rows, :]
    conv = cb_ref[...] + cw_ref[SSM_CONV - 1:SSM_CONV, :] * ext_ref[halo:halo + rows, :]
    for j in range(1, SSM_CONV):
        conv = conv + cw_ref[SSM_CONV - 1 - j:SSM_CONV - j, :] * ext_ref[halo - j:halo - j + rows, :]
    xact = _silu(conv)

    dt = _softplus(dt_ref[0] + dtb_ref[...])
    da = dt * (-jnp.exp(alog_ref[...]))

    ri = lax.broadcasted_iota(jnp.int32, (cl, cl), 0)
    ci = lax.broadcasted_iota(jnp.int32, (cl, cl), 1)
    tri = ri >= ci
    tril = tri.astype(BF16)
    eh = lax.broadcasted_iota(jnp.int32, (DT_PAD, D_SSM), 0)
    el = lax.broadcasted_iota(jnp.int32, (DT_PAD, D_SSM), 1) // HEAD_DIM
    expand = (eh == el).astype(BF16)
    hm_g = _head_masks(gw, F32)

    for c in range(chunks):
        r0 = c * cl
        xs = xact[r0:r0 + cl, 0:D_SSM]
        bm = xact[r0:r0 + cl, D_SSM:D_SSM + SSM_GROUPS * SSM_STATE]
        cm = xact[r0:r0 + cl, D_SSM + SSM_GROUPS * SSM_STATE:]
        acs = _dot_f32_rhs(tril, da[r0:r0 + cl])
        acs_t = acs.T
        acs_full = _dot_f32_lhs(acs, expand)
        dt_full = _dot_f32_lhs(dt[r0:r0 + cl], expand)
        xd = xs * dt_full
        last = acs_full[cl - 1:cl, :]
        xdd = xd * jnp.exp(last - acs_full)
        exp_acs = jnp.exp(acs_full)
        chunk_decay = jnp.exp(last)
        y_parts = []
        for g in range(SSM_GROUPS):
            gs = slice(g * gw, (g + 1) * gw)
            bg = bm[:, g * SSM_STATE:(g + 1) * SSM_STATE].astype(BF16)
            cg = cm[:, g * SSM_STATE:(g + 1) * SSM_STATE].astype(BF16)
            cb = _dot_nt(cg, bg)
            s_in = state_ref[:, gs]
            y_g = _dot(cg, s_in.astype(BF16)) * exp_acs[:, gs]
            state_ref[:, gs] = s_in * chunk_decay[:, gs] + _dot_tn(bg, xdd[:, gs].astype(BF16))
            xd_g = xd[:, gs]
            for hh in range(hpg):
                h = g * hpg + hh
                diff = acs[:, h:h + 1] - acs_t[h:h + 1, :]
                lmat = jnp.where(tri, jnp.exp(jnp.where(tri, diff, 0.0)), 0.0)
                y_g = y_g + _dot((cb * lmat).astype(BF16), (xd_g * hm_g[hh]).astype(BF16))
            y_parts.append(y_g)
        y = jnp.concatenate(y_parts, axis=1) + dsk_ref[...] * xs
        y = y * _silu(z_ref[0, r0:r0 + cl, :])
        outs = []
        for g in range(SSM_GROUPS):
            yg = y[:, g * gw:(g + 1) * gw]
            ms = jnp.mean(yg * yg, axis=-1, keepdims=True)
            outs.append(yg * lax.rsqrt(ms + EPS) * nw_ref[:, g * gw:(g + 1) * gw])
        y_ref[0, r0:r0 + cl, :] = jnp.concatenate(outs, axis=1).astype(BF16)


def _ssd(xbc, z, dt, conv_w, conv_b, dt_bias, a_log, d_skip_full, norm_w, *, chunks=4):
    b, s, _ = xbc.shape
    rows = chunks * SSM_CHUNK
    blk = lambda width: pl.BlockSpec((1, rows, width), lambda bi, ci: (bi, ci, 0))
    const = lambda shape: pl.BlockSpec(shape, lambda bi, ci: (0, 0))
    return pl.pallas_call(
        functools.partial(_ssd_kernel, chunks=chunks),
        grid=(b, s // rows),
        in_specs=[blk(CONV_DIM), blk(D_SSM), blk(DT_PAD),
                  const((SSM_CONV, CONV_DIM)), const((1, CONV_DIM)), const((1, DT_PAD)),
                  const((1, DT_PAD)), const((1, D_SSM)), const((1, D_SSM))],
        out_specs=blk(D_SSM),
        out_shape=jax.ShapeDtypeStruct((b, s, D_SSM), BF16),
        scratch_shapes=[pltpu.VMEM((rows + 8, CONV_DIM), F32),
                        pltpu.VMEM((8, CONV_DIM), F32),
                        pltpu.VMEM((SSM_STATE, D_SSM), F32)],
        compiler_params=pltpu.CompilerParams(
            dimension_semantics=("parallel", "arbitrary"), vmem_limit_bytes=VMEM_LIMIT),
        name="ssd",
    )(xbc, z, dt, conv_w, conv_b, dt_bias, a_log, d_skip_full, norm_w)


def _moba_kernel(q_ref, k_ref, v_ref, o_ref, kmean_ref, acc_ref, m_ref, l_ref, *, n_blk):
    nh = N_ATTN_HEADS
    bs = MOBA_BLOCK
    rows = nh * bs
    qb = pl.program_id(1)

    @pl.when(qb == 0)
    def _():
        kmean_ref[...] = jnp.zeros_like(kmean_ref)
        for n in range(n_blk):
            kblk = k_ref[0, n * bs:(n + 1) * bs, :].astype(F32)
            kmean_ref[n:n + 1, :] = jnp.sum(kblk, axis=0, keepdims=True) * (1.0 / bs)

    hm_b = _head_masks(ATTN_WIDTH, BF16)
    hm_f = _head_masks(ATTN_WIDTH, F32)
    slope = _row_slopes(bs, SLOPES_C)
    q = q_ref[0]
    q4 = jnp.concatenate([q * hm_b[h] for h in range(nh)], axis=0)

    km_hi = kmean_ref[...].astype(BF16)
    km_lo = (kmean_ref[...] - km_hi.astype(F32)).astype(BF16)
    gate = _dot_nt(q4, km_hi) + _dot_nt(q4, km_lo)
    lane = lax.broadcasted_iota(jnp.int32, (rows, LANES), 1)
    lane_f = lane.astype(F32)
    past = lane < qb
    g = jnp.where(past, gate, NEG_INF)
    sel = jnp.zeros((rows, LANES), jnp.bool_)
    for _ in range(MOBA_TOPK):
        mx = jnp.max(g, axis=-1, keepdims=True)
        first = jnp.min(jnp.where(g == mx, lane_f, float(LANES)), axis=-1, keepdims=True)
        pick = lane_f == first
        sel = sel | pick
        g = jnp.where(pick, -jnp.inf, g)
    sel = sel & past
    blk_bias = jnp.where(sel, -(slope * float(bs)) * (qb - lane).astype(F32), NEG_INF).astype(BF16)

    a = lax.broadcasted_iota(jnp.int32, (rows, bs), 0) % bs
    c = lax.broadcasted_iota(jnp.int32, (rows, bs), 1)
    rel = -slope * (a - c).astype(F32)

    s = _dot_nt(q4, k_ref[0, pl.ds(pl.multiple_of(qb * bs, bs), bs), :]) + jnp.where(a >= c, rel, NEG_INF)
    m0 = jnp.max(s, axis=-1, keepdims=True)
    p = jnp.exp(s - m0)
    m_ref[...] = m0
    l_ref[...] = jnp.sum(p, axis=-1, keepdims=True)
    acc_ref[...] = _dot(p.astype(BF16), v_ref[0, pl.ds(pl.multiple_of(qb * bs, bs), bs), :])

    brow = lax.broadcasted_iota(jnp.int32, (LANES, bs), 0)

    def body(kb, carry):
        k0 = pl.multiple_of(kb * bs, bs)
        onehot = (brow == kb).astype(BF16)
        s = _dot_nt(q4, k_ref[0, pl.ds(k0, bs), :]) + rel + _dot(blk_bias, onehot)
        m_old = m_ref[...]
        m_new = jnp.maximum(m_old, jnp.max(s, axis=-1, keepdims=True))
        alpha = jnp.exp(m_old - m_new)
        p = jnp.exp(s - m_new)
        l_ref[...] = alpha * l_ref[...] + jnp.sum(p, axis=-1, keepdims=True)
        acc_ref[...] = alpha * acc_ref[...] + _dot(p.astype(BF16), v_ref[0, pl.ds(k0, bs), :])
        m_ref[...] = m_new
        return carry

    lax.fori_loop(0, qb, body, 0)

    inv = 1.0 / l_ref[...]
    out = jnp.zeros((bs, ATTN_WIDTH), F32)
    for h in range(nh):
        sl = slice(h * bs, (h + 1) * bs)
        out = out + (acc_ref[sl, :] * inv[sl]) * hm_f[h]
    o_ref[0] = out.astype(BF16)


def _moba(qkv):
    b, s, _ = qkv.shape
    w = ATTN_WIDTH
    n_blk = s // MOBA_BLOCK
    rows = N_ATTN_HEADS * MOBA_BLOCK
    return pl.pallas_call(
        functools.partial(_moba_kernel, n_blk=n_blk),
        grid=(b, n_blk),
        in_specs=[pl.BlockSpec((1, MOBA_BLOCK, w), lambda bi, qi: (bi, qi, 0)),
                  pl.BlockSpec((1, s, w), lambda bi, qi: (bi, 0, 1)),
                  pl.BlockSpec((1, s, w), lambda bi, qi: (bi, 0, 2))],
        out_specs=pl.BlockSpec((1, MOBA_BLOCK, w), lambda bi, qi: (bi, qi, 0)),
        out_shape=jax.ShapeDtypeStruct((b, s, w), BF16),
        scratch_shapes=[pltpu.VMEM((LANES, w), F32),
                        pltpu.VMEM((rows, w), F32),
                        pltpu.VMEM((rows, 1), F32),
                        pltpu.VMEM((rows, 1), F32)],
        compiler_params=pltpu.CompilerParams(
            dimension_semantics=("parallel", "arbitrary"), vmem_limit_bytes=VMEM_LIMIT),
        name="moba",
    )(qkv, qkv, qkv)


def _out_mlp_kernel(x_ref, o1_ref, o2_ref, o3_ref, l1_ref, l2_ref, l3_ref, yb_ref, yc_ref,
                    wo_ref, n2_ref, w1_ref, w2_ref, out_ref, x1_ref, h_ref, acc_ref):
    k = pl.program_id(1)

    @pl.when(k == 0)
    def _():
        l1, l2, l3 = l1_ref[...], l2_ref[...], l3_ref[...]
        mx = jnp.maximum(jnp.maximum(l1, l2), l3)
        e1, e2, e3 = jnp.exp(l1 - mx), jnp.exp(l2 - mx), jnp.exp(l3 - mx)
        ya = (e1 * o1_ref[...] + e2 * o2_ref[...] + e3 * o3_ref[...]) / (e1 + e2 + e3)
        wa = ATTN_WIDTH
        x1 = (x_ref[...]
              + _dot(ya.astype(BF16), wo_ref[0:wa, :])
              + _dot(yb_ref[...], wo_ref[wa:wa + D_SSM, :])
              + _dot(yc_ref[...], wo_ref[wa + D_SSM:, :]))
        x1_ref[...] = x1
        ms = jnp.mean(x1 * x1, axis=-1, keepdims=True)
        h_ref[...] = (x1 * lax.rsqrt(ms + EPS) * n2_ref[...]).astype(BF16)
        acc_ref[...] = jnp.zeros_like(acc_ref)

    u = jnp.maximum(_dot(h_ref[...], w1_ref[...]), 0.0)
    acc_ref[...] += _dot((u * u).astype(BF16), w2_ref[...])

    @pl.when(k == pl.num_programs(1) - 1)
    def _():
        out_ref[...] = x1_ref[...] + acc_ref[...]


def _out_mlp(x2d, outs, lses, yb, yc, w_out, n2, w1, w2, *, tm=512, tf=1024):
    t = x2d.shape[0]
    row = lambda width: pl.BlockSpec((tm, width), lambda i, k: (i, 0))
    return pl.pallas_call(
        _out_mlp_kernel,
        grid=(t // tm, D_FF // tf),
        in_specs=[row(D_MODEL)] + [row(ATTN_WIDTH)] * 6 + [row(D_SSM), row(ATTN_WIDTH),
                  pl.BlockSpec((D_MODEL, D_MODEL), lambda i, k: (0, 0)),
                  pl.BlockSpec((1, D_MODEL), lambda i, k: (0, 0)),
                  pl.BlockSpec((D_MODEL, tf), lambda i, k: (0, k)),
                  pl.BlockSpec((tf, D_MODEL), lambda i, k: (k, 0))],
        out_specs=row(D_MODEL),
        out_shape=jax.ShapeDtypeStruct((t, D_MODEL), F32),
        scratch_shapes=[pltpu.VMEM((tm, D_MODEL), F32),
                        pltpu.VMEM((tm, D_MODEL), BF16),
                        pltpu.VMEM((tm, D_MODEL), F32)],
        compiler_params=pltpu.CompilerParams(
            dimension_semantics=("parallel", "arbitrary"), vmem_limit_bytes=VMEM_LIMIT),
        name="out_mlp",
    )(x2d, *outs, *lses, yb, yc, w_out, n2, w1, w2)


def _layer(x2d, bsz, seq, norm1_w, w_in, a_q_norm, a_k_norm, c_q_norm, c_k_norm, conv_w, conv_b,
           dt_bias, a_log, d_skip, ssm_norm_w, w_out, norm2_w, w_mlp_in, w_mlp_out):
    w_pad = jnp.pad(w_in, ((0, 0), (0, DT_PAD - SSM_HEADS))).astype(BF16)
    scale = HEAD_DIM ** -0.5
    gains = jnp.stack([jnp.tile(a_q_norm, N_ATTN_HEADS) * scale, jnp.tile(a_k_norm, N_ATTN_HEADS),
                       jnp.tile(c_q_norm, N_ATTN_HEADS) * scale, jnp.tile(c_k_norm, N_ATTN_HEADS)])
    lane = jnp.arange(ATTN_WIDTH) // HEAD_DIM
    bd = (lane[:, None] == lane[None, :]).astype(BF16)
    pad_h = lambda v: jnp.pad(v, (0, DT_PAD - SSM_HEADS)).reshape(1, DT_PAD)

    qkva, qkvc, z, xbc, dt = _in_proj(x2d, norm1_w.reshape(1, -1), w_pad, gains, bd)

    qkva = qkva.reshape(bsz, seq, -1)
    outs, lses = [], []
    for _, dil in DILATED_PATTERNS:
        o, l = _band_attn(qkva, dil)
        outs.append(o.reshape(bsz * seq, ATTN_WIDTH))
        lses.append(l.reshape(bsz * seq, ATTN_WIDTH))

    yb = _ssd(xbc.reshape(bsz, seq, -1), z.reshape(bsz, seq, -1), dt.reshape(bsz, seq, -1),
              conv_w, conv_b.reshape(1, -1), pad_h(dt_bias), pad_h(a_log),
              jnp.repeat(d_skip, HEAD_DIM).reshape(1, -1), ssm_norm_w.reshape(1, -1))
    yc = _moba(qkvc.reshape(bsz, seq, -1))

    return _out_mlp(x2d, outs, lses, yb.reshape(bsz * seq, -1), yc.reshape(bsz * seq, -1),
                    w_out.astype(BF16), norm2_w.reshape(1, -1),
                    w_mlp_in.astype(BF16), w_mlp_out.astype(BF16))


def kernel(x, norm1_w, w_in, a_q_norm, a_k_norm, c_q_norm, c_k_norm, conv_w, conv_b, dt_bias,
           a_log, d_skip, ssm_norm_w, w_out, norm2_w, w_mlp_in, w_mlp_out):
    bsz, seq, d = x.shape
    x2d = x.reshape(bsz * seq, d)
    for i in range(norm1_w.shape[0]):
        x2d = _layer(x2d, bsz, seq, norm1_w[i], w_in[i], a_q_norm[i], a_k_norm[i], c_q_norm[i],
                     c_k_norm[i], conv_w[i], conv_b[i], dt_bias[i], a_log[i], d_skip[i],
                     ssm_norm_w[i], w_out[i], norm2_w[i], w_mlp_in[i], w_mlp_out[i])
    return x2d.reshape(bsz, seq, d)
```

```python
import functools

import jax
import jax.numpy as jnp
from jax import lax
from jax.experimental import pallas as pl
from jax.experimental.pallas import tpu as pltpu

F32 = jnp.float32
BF16 = jnp.bfloat16

D_MODEL = 1024
HEAD_DIM = 64
N_ATTN_HEADS = 4
ATTN_WIDTH = N_ATTN_HEADS * HEAD_DIM
DILATED_PATTERNS = ((128, 1), (512, 4), (2048, 16))
BAND = 128
MOBA_BLOCK = 256
MOBA_TOPK = 3
SSM_HEADS = 8
D_SSM = SSM_HEADS * HEAD_DIM
SSM_STATE = 128
SSM_GROUPS = 2
SSM_CONV = 4
SSM_CHUNK = 128
CONV_DIM = D_SSM + 2 * SSM_GROUPS * SSM_STATE
D_FF = 4 * D_MODEL
D_IN = 2 * 3 * ATTN_WIDTH + D_SSM + CONV_DIM + SSM_HEADS
LANES = 128
DT_PAD = LANES
D_IN_PAD = D_IN - SSM_HEADS + DT_PAD
EPS = 1e-6
NEG_INF = -1e30
SLOPES_A = tuple(2.0 ** -(2 * h + 1) for h in range(N_ATTN_HEADS))
SLOPES_C = tuple(2.0 ** -(2 * h + 2) for h in range(N_ATTN_HEADS))

VMEM_LIMIT = 56 * 1024 * 1024


def _split3(a):
    hi = a.astype(BF16)
    r1 = a - hi.astype(F32)
    mid = r1.astype(BF16)
    lo = (r1 - mid.astype(F32)).astype(BF16)
    return hi, mid, lo


def _dot(a, b):
    return jnp.dot(a, b, preferred_element_type=F32)


def _dot_nt(a, b):
    return lax.dot_general(a, b, (((1,), (1,)), ((), ())), preferred_element_type=F32)


def _dot_tn(a, b):
    return lax.dot_general(a, b, (((0,), (0,)), ((), ())), preferred_element_type=F32)


def _dot_f32_lhs(a, sel):
    hi, mid, lo = _split3(a)
    return _dot(hi, sel) + _dot(mid, sel) + _dot(lo, sel)


def _dot_f32_rhs(sel, b):
    hi, mid, lo = _split3(b)
    return _dot(sel, hi) + _dot(sel, mid) + _dot(sel, lo)


def _head_masks(width, dtype):
    lane = lax.broadcasted_iota(jnp.int32, (1, width), 1)
    return [(lane // HEAD_DIM == h).astype(dtype) for h in range(width // HEAD_DIM)]


def _row_slopes(rows_per_head, slopes):
    row = lax.broadcasted_iota(jnp.int32, (len(slopes) * rows_per_head, 1), 0) // rows_per_head
    out = jnp.full(row.shape, slopes[-1], F32)
    for h in range(len(slopes) - 2, -1, -1):
        out = jnp.where(row == h, slopes[h], out)
    return out


def _in_proj_kernel(x_ref, n1_ref, w_ref, g_ref, bd_ref,
                    qkva_ref, qkvc_ref, z_ref, xbc_ref, dt_ref):
    x = x_ref[...]
    ms = jnp.mean(x * x, axis=-1, keepdims=True)
    h = (x * lax.rsqrt(ms + EPS) * n1_ref[...]).astype(BF16)
    bd = bd_ref[...]

    def seg(lo, width):
        return _dot(h, w_ref[:, lo:lo + width])

    def head_norm(y, g):
        y2 = y * y
        hi = y2.astype(BF16)
        lo = (y2 - hi.astype(F32)).astype(BF16)
        ss = _dot(hi, bd) + _dot(lo, bd)
        return y * lax.rsqrt(ss * (1.0 / HEAD_DIM) + EPS) * g

    w = ATTN_WIDTH
    for i, out_ref in enumerate((qkva_ref, qkvc_ref)):
        base = 3 * w * i
        out_ref[:, 0:w] = head_norm(seg(base, w), g_ref[2 * i:2 * i + 1, :]).astype(BF16)
        out_ref[:, w:2 * w] = head_norm(seg(base + w, w), g_ref[2 * i + 1:2 * i + 2, :]).astype(BF16)
        out_ref[:, 2 * w:3 * w] = seg(base + 2 * w, w).astype(BF16)
    z_ref[...] = seg(6 * w, D_SSM)
    xbc_ref[...] = seg(6 * w + D_SSM, CONV_DIM)
    dt_ref[...] = seg(6 * w + D_SSM + CONV_DIM, DT_PAD)


def _in_proj(x2d, n1, w_pad, gains, bd, *, tm=512):
    t = x2d.shape[0]
    row = lambda i: (i, 0)
    const = lambda i: (0, 0)
    return pl.pallas_call(
        _in_proj_kernel,
        grid=(t // tm,),
        in_specs=[
            pl.BlockSpec((tm, D_MODEL), row),
            pl.BlockSpec((1, D_MODEL), const),
            pl.BlockSpec((D_MODEL, D_IN_PAD), const),
            pl.BlockSpec((4, ATTN_WIDTH), const),
            pl.BlockSpec((ATTN_WIDTH, ATTN_WIDTH), const),
        ],
        out_specs=[
            pl.BlockSpec((tm, 3 * ATTN_WIDTH), row),
            pl.BlockSpec((tm, 3 * ATTN_WIDTH), row),
            pl.BlockSpec((tm, D_SSM), row),
            pl.BlockSpec((tm, CONV_DIM), row),
            pl.BlockSpec((tm, DT_PAD), row),
        ],
        out_shape=[
            jax.ShapeDtypeStruct((t, 3 * ATTN_WIDTH), BF16),
            jax.ShapeDtypeStruct((t, 3 * ATTN_WIDTH), BF16),
            jax.ShapeDtypeStruct((t, D_SSM), F32),
            jax.ShapeDtypeStruct((t, CONV_DIM), F32),
            jax.ShapeDtypeStruct((t, DT_PAD), F32),
        ],
        compiler_params=pltpu.CompilerParams(
            dimension_semantics=("parallel",), vmem_limit_bytes=VMEM_LIMIT),
        name="in_proj",
    )(x2d, n1, w_pad, gains, bd)


def _band_attn_kernel(q_ref, k_ref, v_ref, o_ref, lse_ref, *, dil, length):
    nh = N_ATTN_HEADS
    rows = nh * BAND
    hm_b = _head_masks(ATTN_WIDTH, BF16)
    hm_f = _head_masks(ATTN_WIDTH, F32)
    slope = _row_slopes(BAND, SLOPES_A)
    a = lax.broadcasted_iota(jnp.int32, (rows, 2 * BAND), 0) % BAND
    c = lax.broadcasted_iota(jnp.int32, (rows, 2 * BAND), 1)

    def bias_for(offset):
        valid = (offset >= 0) & (offset <= BAND)
        return jnp.where(valid, -(slope * float(dil)) * offset.astype(F32), NEG_INF)

    bias_first = bias_for(a - c)
    bias_rest = bias_for(a + BAND - c)

    def tile(q0, k0, bias):
        q = q_ref[0, pl.ds(q0, BAND), :]
        q4 = jnp.concatenate([q * hm_b[h] for h in range(nh)], axis=0)
        kw = k_ref[0, pl.ds(k0, 2 * BAND), :]
        vw = v_ref[0, pl.ds(k0, 2 * BAND), :]
        s = _dot_nt(q4, kw) + bias
        mx = jnp.max(s, axis=-1, keepdims=True)
        p = jnp.exp(s - mx)
        l = jnp.sum(p, axis=-1, keepdims=True)
        o4 = _dot(p.astype(BF16), vw)
        inv = 1.0 / l
        lse = mx + jnp.log(l)
        out = jnp.zeros((BAND, ATTN_WIDTH), F32)
        lse_full = jnp.zeros((BAND, ATTN_WIDTH), F32)
        for h in range(nh):
            sl = slice(h * BAND, (h + 1) * BAND)
            out = out + (o4[sl] * inv[sl]) * hm_f[h]
            lse_full = lse_full + lse[sl] * hm_f[h]
        o_ref[0, pl.ds(q0, BAND), :] = out
        lse_ref[0, pl.ds(q0, BAND), :] = lse_full

    tile(0, 0, bias_first)

    def body(m, carry):
        q0 = pl.multiple_of(m * BAND, BAND)
        tile(q0, pl.multiple_of(q0 - BAND, BAND), bias_rest)
        return carry

    lax.fori_loop(1, length // BAND, body, 0)


def _band_attn(qkv, dil):
    b, s, _ = qkv.shape
    length = s // dil
    w = ATTN_WIDTH
    view = qkv.reshape(b, length, dil * 3 * w)

    def in_spec(j):
        return pl.BlockSpec((1, length, w), lambda bi, r: (bi, 0, 3 * r + j))

    out_spec = pl.BlockSpec((1, length, w), lambda bi, r: (bi, 0, r))
    out, lse = pl.pallas_call(
        functools.partial(_band_attn_kernel, dil=dil, length=length),
        grid=(b, dil),
        in_specs=[in_spec(0), in_spec(1), in_spec(2)],
        out_specs=[out_spec, out_spec],
        out_shape=[jax.ShapeDtypeStruct((b, length, dil * w), F32)] * 2,
        compiler_params=pltpu.CompilerParams(
            dimension_semantics=("parallel", "parallel"), vmem_limit_bytes=VMEM_LIMIT),
        name=f"band_attn_d{dil}",
    )(view, view, view)
    return out.reshape(b, s, w), lse.reshape(b, s, w)


def _softplus(x):
    return jnp.maximum(x, 0.0) + jnp.log1p(jnp.exp(-jnp.abs(x)))


def _silu(x):
    return x / (1.0 + jnp.exp(-x))


def _ssd_kernel(xbc_ref, z_ref, dt_ref, cw_ref, cb_ref, dtb_ref, alog_ref, dsk_ref, nw_ref,
                y_ref, ext_ref, tail_ref, state_ref, *, chunks):
    rows = chunks * SSM_CHUNK
    halo = 8
    cl = SSM_CHUNK
    gw = D_SSM // SSM_GROUPS
    hpg = SSM_HEADS // SSM_GROUPS

    @pl.when(pl.program_id(1) == 0)
    def _():
        tail_ref[...] = jnp.zeros_like(tail_ref)
        state_ref[...] = jnp.zeros_like(state_ref)

    ext_ref[0:halo, :] = tail_ref[...]
    ext_ref[halo:halo + rows, :] = xbc_ref[0]
    tail_ref[...] = xbc_ref[0, rows - halo:rows, :]
    conv = cb_ref[...] + cw_ref[SSM_CONV - 1:SSM_CONV, :] * ext_ref[halo:halo + rows, :]
    for j in range(1, SSM_CONV):
        conv = conv + cw_ref[SSM_CONV - 1 - j:SSM_CONV - j, :] * ext_ref[halo - j:halo - j + rows, :]
    xact = _silu(conv)

    dt = _softplus(dt_ref[0] + dtb_ref[...])
    da = dt * (-jnp.exp(alog_ref[...]))

    ri = lax.broadcasted_iota(jnp.int32, (cl, cl), 0)
    ci = lax.broadcasted_iota(jnp.int32, (cl, cl), 1)
    tri = ri >= ci
    tril = tri.astype(BF16)
    eh = lax.broadcasted_iota(jnp.int32, (DT_PAD, D_SSM), 0)
    el = lax.broadcasted_iota(jnp.int32, (DT_PAD, D_SSM), 1) // HEAD_DIM
    expand = (eh == el).astype(BF16)
    hm_g = _head_masks(gw, F32)

    for c in range(chunks):
        r0 = c * cl
        xs = xact[r0:r0 + cl, 0:D_SSM]
        bm = xact[r0:r0 + cl, D_SSM:D_SSM + SSM_GROUPS * SSM_STATE]
        cm = xact[r0:r0 + cl, D_SSM + SSM_GROUPS * SSM_STATE:]
        acs = _dot_f32_rhs(tril, da[r0:r0 + cl])
        acs_t = acs.T
        acs_full = _dot_f32_lhs(acs, expand)
        dt_full = _dot_f32_lhs(dt[r0:r0 + cl], expand)
        xd = xs * dt_full
        last = acs_full[cl - 1:cl, :]
        xdd = xd * jnp.exp(last - acs_full)
        exp_acs = jnp.exp(acs_full)
        chunk_decay = jnp.exp(last)
        y_parts = []
        for g in range(SSM_GROUPS):
            gs = slice(g * gw, (g + 1) * gw)
            bg = bm[:, g * SSM_STATE:(g + 1) * SSM_STATE].astype(BF16)
            cg = cm[:, g * SSM_STATE:(g + 1) * SSM_STATE].astype(BF16)
            cb = _dot_nt(cg, bg)
            s_in = state_ref[:, gs]
            y_g = _dot(cg, s_in.astype(BF16)) * exp_acs[:, gs]
            state_ref[:, gs] = s_in * chunk_decay[:, gs] + _dot_tn(bg, xdd[:, gs].astype(BF16))
            xd_g = xd[:, gs]
            for hh in range(hpg):
                h = g * hpg + hh
                diff = acs[:, h:h + 1] - acs_t[h:h + 1, :]
                lmat = jnp.where(tri, jnp.exp(jnp.where(tri, diff, 0.0)), 0.0)
                y_g = y_g + _dot((cb * lmat).astype(BF16), (xd_g * hm_g[hh]).astype(BF16))
            y_parts.append(y_g)
        y = jnp.concatenate(y_parts, axis=1) + dsk_ref[...] * xs
        y = y * _silu(z_ref[0, r0:r0 + cl, :])
        outs = []
        for g in range(SSM_GROUPS):
            yg = y[:, g * gw:(g + 1) * gw]
            ms = jnp.mean(yg * yg, axis=-1, keepdims=True)
            outs.append(yg * lax.rsqrt(ms + EPS) * nw_ref[:, g * gw:(g + 1) * gw])
        y_ref[0, r0:r0 + cl, :] = jnp.concatenate(outs, axis=1).astype(BF16)


def _ssd(xbc, z, dt, conv_w, conv_b, dt_bias, a_log, d_skip_full, norm_w, *, chunks=4):
    b, s, _ = xbc.shape
    rows = chunks * SSM_CHUNK
    blk = lambda width: pl.BlockSpec((1, rows, width), lambda bi, ci: (bi, ci, 0))
    const = lambda shape: pl.BlockSpec(shape, lambda bi, ci: (0, 0))
    return pl.pallas_call(
        functools.partial(_ssd_kernel, chunks=chunks),
        grid=(b, s // rows),
        in_specs=[blk(CONV_DIM), blk(D_SSM), blk(DT_PAD),
                  const((SSM_CONV, CONV_DIM)), const((1, CONV_DIM)), const((1, DT_PAD)),
                  const((1, DT_PAD)), const((1, D_SSM)), const((1, D_SSM))],
        out_specs=blk(D_SSM),
        out_shape=jax.ShapeDtypeStruct((b, s, D_SSM), BF16),
        scratch_shapes=[pltpu.VMEM((rows + 8, CONV_DIM), F32),
                        pltpu.VMEM((8, CONV_DIM), F32),
                        pltpu.VMEM((SSM_STATE, D_SSM), F32)],
        compiler_params=pltpu.CompilerParams(
            dimension_semantics=("parallel", "arbitrary"), vmem_limit_bytes=VMEM_LIMIT),
        name="ssd",
    )(xbc, z, dt, conv_w, conv_b, dt_bias, a_log, d_skip_full, norm_w)


MOBA_QT = LANES


def _moba_kernel(q_ref, k_ref, v_ref, o_ref, kmean_ref, vt_ref, rel_ref, relown_ref, q4t_ref,
                 bias_ref, sa_ref, sb_ref, acc_ref, m_ref, l_ref, *, n_blk):
    nh = N_ATTN_HEADS
    bs = MOBA_BLOCK
    cols = nh * bs
    qb = pl.program_id(1)

    col = lax.broadcasted_iota(jnp.int32, (1, cols), 1)
    col_head = col // bs
    slope = jnp.full((1, cols), SLOPES_C[-1], F32)
    for h in range(nh - 2, -1, -1):
        slope = jnp.where(col_head == h, SLOPES_C[h], slope)

    @pl.when(qb == 0)
    def _():
        for n in range(n_blk):
            kblk = k_ref[0, n * bs:(n + 1) * bs, :].astype(F32)
            kmean_ref[n:n + 1, :] = jnp.sum(kblk, axis=0, keepdims=True) * (1.0 / bs)
            vt_ref[n] = v_ref[0, n * bs:(n + 1) * bs, :].astype(F32).T.astype(BF16)
        c = lax.broadcasted_iota(jnp.int32, (bs, cols), 0)
        a = lax.broadcasted_iota(jnp.int32, (bs, cols), 1) % bs
        rel = -slope * (a - c).astype(F32)
        rel_ref[...] = rel
        relown_ref[...] = jnp.where(a >= c, rel, NEG_INF)

    qt = q_ref[0].astype(F32).T
    frow = lax.broadcasted_iota(jnp.int32, (ATTN_WIDTH, bs), 0) // HEAD_DIM
    for h in range(nh):
        q4t_ref[:, h * bs:(h + 1) * bs] = jnp.where(frow == h, qt, 0.0).astype(BF16)

    km = kmean_ref[...]
    km_hi = km.astype(BF16)
    km_lo = (km - km_hi.astype(F32)).astype(BF16)
    q4t = q4t_ref[...]
    gate = _dot(km_hi, q4t) + _dot(km_lo, q4t)
    blk = lax.broadcasted_iota(jnp.int32, (n_blk, cols), 0)
    blk_f = blk.astype(F32)
    past = blk < qb
    g = jnp.where(past, gate, NEG_INF)
    sel = jnp.zeros((n_blk, cols), jnp.bool_)
    for _ in range(MOBA_TOPK):
        mx = jnp.max(g, axis=0, keepdims=True)
        first = jnp.min(jnp.where(g == mx, blk_f, float(n_blk)), axis=0, keepdims=True)
        pick = blk_f == first
        sel = sel | pick
        g = jnp.where(pick, -jnp.inf, g)
    sel = sel & past
    bias_ref[...] = jnp.where(sel, -(slope * float(bs)) * (qb - blk).astype(F32), NEG_INF)

    def scores(kb, dst_ref):
        kblk = k_ref[0, pl.ds(pl.multiple_of(kb * bs, bs), bs), :]
        for h in range(nh):
            hs = slice(h * bs, (h + 1) * bs)
            dst_ref[:, hs] = _dot(kblk, q4t_ref[:, hs])

    def attend(kb, src_ref, rel_mat_ref, bias_row):
        first = bias_row is None
        for h in range(nh):
            hs = slice(h * bs, (h + 1) * bs)
            ps, alphas = [], []
            for t in range(h * bs // MOBA_QT, (h + 1) * bs // MOBA_QT):
                cs = slice(t * MOBA_QT, (t + 1) * MOBA_QT)
                s = src_ref[:, cs] + rel_mat_ref[:, cs]
                mx = jnp.max(s, axis=0, keepdims=True)
                if first:
                    m_new = shift = mx
                else:
                    b = bias_row[:, cs]
                    m_old = m_ref[t:t + 1, :]
                    m_new = jnp.maximum(m_old, mx + b)
                    shift = m_new - b
                    alphas.append(jnp.exp(m_old - m_new))
                p = jnp.exp(s - shift)
                lsum = jnp.sum(p, axis=0, keepdims=True)
                l_ref[t:t + 1, :] = lsum if first else alphas[-1] * l_ref[t:t + 1, :] + lsum
                m_ref[t:t + 1, :] = m_new
                ps.append(p.astype(BF16))
            pv = _dot(vt_ref[kb, h * HEAD_DIM:(h + 1) * HEAD_DIM, :], jnp.concatenate(ps, axis=1))
            acc_ref[:, hs] = pv if first else jnp.concatenate(alphas, axis=1) * acc_ref[:, hs] + pv

    scores(qb, sa_ref)
    scores(0, sb_ref)
    attend(qb, sa_ref, relown_ref, None)

    def body(i, carry):
        kb = i - 1
        bias_row = bias_ref[pl.ds(kb, 1), :]

        @pl.when(i % 2 == 1)
        def _():
            scores(i, sa_ref)
            attend(kb, sb_ref, rel_ref, bias_row)

        @pl.when(i % 2 == 0)
        def _():
            scores(i, sb_ref)
            attend(kb, sa_ref, rel_ref, bias_row)

        return carry

    lax.fori_loop(1, qb + 1, body, 0)

    inv = 1.0 / l_ref[...]
    heads = []
    for h in range(nh):
        tiles = range(h * bs // MOBA_QT, (h + 1) * bs // MOBA_QT)
        heads.append(jnp.concatenate(
            [acc_ref[:, t * MOBA_QT:(t + 1) * MOBA_QT] * inv[t:t + 1, :] for t in tiles], axis=1))
    o_ref[0] = jnp.concatenate(heads, axis=0).T.astype(BF16)


def _moba(qkv):
    b, s, _ = qkv.shape
    w = ATTN_WIDTH
    bs = MOBA_BLOCK
    n_blk = s // bs
    cols = N_ATTN_HEADS * bs
    return pl.pallas_call(
        functools.partial(_moba_kernel, n_blk=n_blk),
        grid=(b, n_blk),
        in_specs=[pl.BlockSpec((1, bs, w), lambda bi, qi: (bi, qi, 0)),
                  pl.BlockSpec((1, s, w), lambda bi, qi: (bi, 0, 1)),
                  pl.BlockSpec((1, s, w), lambda bi, qi: (bi, 0, 2))],
        out_specs=pl.BlockSpec((1, bs, w), lambda bi, qi: (bi, qi, 0)),
        out_shape=jax.ShapeDtypeStruct((b, s, w), BF16),
        scratch_shapes=[pltpu.VMEM((n_blk, w), F32),
                        pltpu.VMEM((n_blk, w, bs), BF16),
                        pltpu.VMEM((bs, cols), F32),
                        pltpu.VMEM((bs, cols), F32),
                        pltpu.VMEM((w, cols), BF16),
                        pltpu.VMEM((n_blk, cols), F32),
                        pltpu.VMEM((bs, cols), F32),
                        pltpu.VMEM((bs, cols), F32),
                        pltpu.VMEM((HEAD_DIM, cols), F32),
                        pltpu.VMEM((cols // MOBA_QT, MOBA_QT), F32),
                        pltpu.VMEM((cols // MOBA_QT, MOBA_QT), F32)],
        compiler_params=pltpu.CompilerParams(
            dimension_semantics=("parallel", "arbitrary"), vmem_limit_bytes=VMEM_LIMIT),
        name="moba",
    )(qkv, qkv, qkv)


def _out_mlp_kernel(x_ref, o1_ref, o2_ref, o3_ref, l1_ref, l2_ref, l3_ref, yb_ref, yc_ref,
                    wo_ref, n2_ref, w1_ref, w2_ref, out_ref, x1_ref, h_ref, acc_ref):
    k = pl.program_id(1)

    @pl.when(k == 0)
    def _():
        l1, l2, l3 = l1_ref[...], l2_ref[...], l3_ref[...]
        mx = jnp.maximum(jnp.maximum(l1, l2), l3)
        e1, e2, e3 = jnp.exp(l1 - mx), jnp.exp(l2 - mx), jnp.exp(l3 - mx)
        ya = (e1 * o1_ref[...] + e2 * o2_ref[...] + e3 * o3_ref[...]) / (e1 + e2 + e3)
        wa = ATTN_WIDTH
        x1 = (x_ref[...]
              + _dot(ya.astype(BF16), wo_ref[0:wa, :])
              + _dot(yb_ref[...], wo_ref[wa:wa + D_SSM, :])
              + _dot(yc_ref[...], wo_ref[wa + D_SSM:, :]))
        x1_ref[...] = x1
        ms = jnp.mean(x1 * x1, axis=-1, keepdims=True)
        h_ref[...] = (x1 * lax.rsqrt(ms + EPS) * n2_ref[...]).astype(BF16)
        acc_ref[...] = jnp.zeros_like(acc_ref)

    u = jnp.maximum(_dot(h_ref[...], w1_ref[...]), 0.0)
    acc_ref[...] += _dot((u * u).astype(BF16), w2_ref[...])

    @pl.when(k == pl.num_programs(1) - 1)
    def _():
        out_ref[...] = x1_ref[...] + acc_ref[...]


def _out_mlp(x2d, outs, lses, yb, yc, w_out, n2, w1, w2, *, tm=512, tf=1024):
    t = x2d.shape[0]
    row = lambda width: pl.BlockSpec((tm, width), lambda i, k: (i, 0))
    return pl.pallas_call(
        _out_mlp_kernel,
        grid=(t // tm, D_FF // tf),
        in_specs=[row(D_MODEL)] + [row(ATTN_WIDTH)] * 6 + [row(D_SSM), row(ATTN_WIDTH),
                  pl.BlockSpec((D_MODEL, D_MODEL), lambda i, k: (0, 0)),
                  pl.BlockSpec((1, D_MODEL), lambda i, k: (0, 0)),
                  pl.BlockSpec((D_MODEL, tf), lambda i, k: (0, k)),
                  pl.BlockSpec((tf, D_MODEL), lambda i, k: (k, 0))],
        out_specs=row(D_MODEL),
        out_shape=jax.ShapeDtypeStruct((t, D_MODEL), F32),
        scratch_shapes=[pltpu.VMEM((tm, D_MODEL), F32),
                        pltpu.VMEM((tm, D_MODEL), BF16),
                        pltpu.VMEM((tm, D_MODEL), F32)],
        compiler_params=pltpu.CompilerParams(
            dimension_semantics=("parallel", "arbitrary"), vmem_limit_bytes=VMEM_LIMIT),
        name="out_mlp",
    )(x2d, *outs, *lses, yb, yc, w_out, n2, w1, w2)


def _layer(x2d, bsz, seq, norm1_w, w_in, a_q_norm, a_k_norm, c_q_norm, c_k_norm, conv_w, conv_b,
           dt_bias, a_log, d_skip, ssm_norm_w, w_out, norm2_w, w_mlp_in, w_mlp_out):
    w_pad = jnp.pad(w_in, ((0, 0), (0, DT_PAD - SSM_HEADS))).astype(BF16)
    scale = HEAD_DIM ** -0.5
    gains = jnp.stack([jnp.tile(a_q_norm, N_ATTN_HEADS) * scale, jnp.tile(a_k_norm, N_ATTN_HEADS),
                       jnp.tile(c_q_norm, N_ATTN_HEADS) * scale, jnp.tile(c_k_norm, N_ATTN_HEADS)])
    lane = jnp.arange(ATTN_WIDTH) // HEAD_DIM
    bd = (lane[:, None] == lane[None, :]).astype(BF16)
    pad_h = lambda v: jnp.pad(v, (0, DT_PAD - SSM_HEADS)).reshape(1, DT_PAD)

    qkva, qkvc, z, xbc, dt = _in_proj(x2d, norm1_w.reshape(1, -1), w_pad, gains, bd)

    qkva = qkva.reshape(bsz, seq, -1)
    outs, lses = [], []
    for _, dil in DILATED_PATTERNS:
        o, l = _band_attn(qkva, dil)
        outs.append(o.reshape(bsz * seq, ATTN_WIDTH))
        lses.append(l.reshape(bsz * seq, ATTN_WIDTH))

    yb = _ssd(xbc.reshape(bsz, seq, -1), z.reshape(bsz, seq, -1), dt.reshape(bsz, seq, -1),
              conv_w, conv_b.reshape(1, -1), pad_h(dt_bias), pad_h(a_log),
              jnp.repeat(d_skip, HEAD_DIM).reshape(1, -1), ssm_norm_w.reshape(1, -1))
    yc = _moba(qkvc.reshape(bsz, seq, -1))

    return _out_mlp(x2d, outs, lses, yb.reshape(bsz * seq, -1), yc.reshape(bsz * seq, -1),
                    w_out.astype(BF16), norm2_w.reshape(1, -1),
                    w_mlp_in.astype(BF16), w_mlp_out.astype(BF16))


def kernel(x, norm1_w, w_in, a_q_norm, a_k_norm, c_q_norm, c_k_norm, conv_w, conv_b, dt_bias,
           a_log, d_skip, ssm_norm_w, w_out, norm2_w, w_mlp_in, w_mlp_out):
    bsz, seq, d = x.shape
    x2d = x.reshape(bsz * seq, d)
    for i in range(norm1_w.shape[0]):
        x2d = _layer(x2d, bsz, seq, norm1_w[i], w_in[i], a_q_norm[i], a_k_norm[i], c_q_norm[i],
                     c_k_norm[i], conv_w[i], conv_b[i], dt_bias[i], a_log[i], d_skip[i],
                     ssm_norm_w[i], w_out[i], norm2_w[i], w_mlp_in[i], w_mlp_out[i])
    return x2d.reshape(bsz, seq, d)
```

```python
import functools

import jax
import jax.numpy as jnp
from jax import lax
from jax.experimental import pallas as pl
from jax.experimental.pallas import tpu as pltpu

F32 = jnp.float32
BF16 = jnp.bfloat16

D_MODEL = 1024
HEAD_DIM = 64
N_ATTN_HEADS = 4
ATTN_WIDTH = N_ATTN_HEADS * HEAD_DIM
DILATED_PATTERNS = ((128, 1), (512, 4), (2048, 16))
BAND = 128
MOBA_BLOCK = 256
MOBA_TOPK = 3
SSM_HEADS = 8
D_SSM = SSM_HEADS * HEAD_DIM
SSM_STATE = 128
SSM_GROUPS = 2
SSM_CONV = 4
SSM_CHUNK = 128
CONV_DIM = D_SSM + 2 * SSM_GROUPS * SSM_STATE
D_FF = 4 * D_MODEL
D_IN = 2 * 3 * ATTN_WIDTH + D_SSM + CONV_DIM + SSM_HEADS
LANES = 128
DT_PAD = LANES
D_IN_PAD = D_IN - SSM_HEADS + DT_PAD
EPS = 1e-6
NEG_INF = -1e30
SLOPES_A = tuple(2.0 ** -(2 * h + 1) for h in range(N_ATTN_HEADS))
SLOPES_C = tuple(2.0 ** -(2 * h + 2) for h in range(N_ATTN_HEADS))

VMEM_LIMIT = 56 * 1024 * 1024


def _split3(a):
    hi = a.astype(BF16)
    r1 = a - hi.astype(F32)
    mid = r1.astype(BF16)
    lo = (r1 - mid.astype(F32)).astype(BF16)
    return hi, mid, lo


def _dot(a, b):
    return jnp.dot(a, b, preferred_element_type=F32)


def _dot_nt(a, b):
    return lax.dot_general(a, b, (((1,), (1,)), ((), ())), preferred_element_type=F32)


def _dot_tn(a, b):
    return lax.dot_general(a, b, (((0,), (0,)), ((), ())), preferred_element_type=F32)


def _dot_f32_lhs(a, sel):
    hi, mid, lo = _split3(a)
    return _dot(hi, sel) + _dot(mid, sel) + _dot(lo, sel)


def _dot_f32_rhs(sel, b):
    hi, mid, lo = _split3(b)
    return _dot(sel, hi) + _dot(sel, mid) + _dot(sel, lo)


def _head_masks(width, dtype):
    lane = lax.broadcasted_iota(jnp.int32, (1, width), 1)
    return [(lane // HEAD_DIM == h).astype(dtype) for h in range(width // HEAD_DIM)]


def _row_slopes(rows_per_head, slopes):
    row = lax.broadcasted_iota(jnp.int32, (len(slopes) * rows_per_head, 1), 0) // rows_per_head
    out = jnp.full(row.shape, slopes[-1], F32)
    for h in range(len(slopes) - 2, -1, -1):
        out = jnp.where(row == h, slopes[h], out)
    return out


def _in_proj_kernel(x_ref, n1_ref, w_ref, g_ref, bd_ref,
                    qkva_ref, qkvc_ref, z_ref, xbc_ref, dt_ref):
    x = x_ref[...]
    ms = jnp.mean(x * x, axis=-1, keepdims=True)
    h = (x * lax.rsqrt(ms + EPS) * n1_ref[...]).astype(BF16)
    bd = bd_ref[...]

    def seg(lo, width):
        return _dot(h, w_ref[:, lo:lo + width])

    def head_norm(y, g):
        y2 = y * y
        hi = y2.astype(BF16)
        lo = (y2 - hi.astype(F32)).astype(BF16)
        ss = _dot(hi, bd) + _dot(lo, bd)
        return y * lax.rsqrt(ss * (1.0 / HEAD_DIM) + EPS) * g

    w = ATTN_WIDTH
    for i, out_ref in enumerate((qkva_ref, qkvc_ref)):
        base = 3 * w * i
        out_ref[:, 0:w] = head_norm(seg(base, w), g_ref[2 * i:2 * i + 1, :]).astype(BF16)
        out_ref[:, w:2 * w] = head_norm(seg(base + w, w), g_ref[2 * i + 1:2 * i + 2, :]).astype(BF16)
        out_ref[:, 2 * w:3 * w] = seg(base + 2 * w, w).astype(BF16)
    z_ref[...] = seg(6 * w, D_SSM)
    xbc_ref[...] = seg(6 * w + D_SSM, CONV_DIM)
    dt_ref[...] = seg(6 * w + D_SSM + CONV_DIM, DT_PAD)


def _in_proj(x2d, n1, w_pad, gains, bd, *, tm=512):
    t = x2d.shape[0]
    row = lambda i: (i, 0)
    const = lambda i: (0, 0)
    return pl.pallas_call(
        _in_proj_kernel,
        grid=(t // tm,),
        in_specs=[
            pl.BlockSpec((tm, D_MODEL), row),
            pl.BlockSpec((1, D_MODEL), const),
            pl.BlockSpec((D_MODEL, D_IN_PAD), const),
            pl.BlockSpec((4, ATTN_WIDTH), const),
            pl.BlockSpec((ATTN_WIDTH, ATTN_WIDTH), const),
        ],
        out_specs=[
            pl.BlockSpec((tm, 3 * ATTN_WIDTH), row),
            pl.BlockSpec((tm, 3 * ATTN_WIDTH), row),
            pl.BlockSpec((tm, D_SSM), row),
            pl.BlockSpec((tm, CONV_DIM), row),
            pl.BlockSpec((tm, DT_PAD), row),
        ],
        out_shape=[
            jax.ShapeDtypeStruct((t, 3 * ATTN_WIDTH), BF16),
            jax.ShapeDtypeStruct((t, 3 * ATTN_WIDTH), BF16),
            jax.ShapeDtypeStruct((t, D_SSM), F32),
            jax.ShapeDtypeStruct((t, CONV_DIM), F32),
            jax.ShapeDtypeStruct((t, DT_PAD), F32),
        ],
        compiler_params=pltpu.CompilerParams(
            dimension_semantics=("parallel",), vmem_limit_bytes=VMEM_LIMIT),
        name="in_proj",
    )(x2d, n1, w_pad, gains, bd)


N_DIL_STEPS = sum(d for _, d in DILATED_PATTERNS)
N_SLABS = 3 * ATTN_WIDTH // LANES
COPY_ROWS = 512


def _dilated_kernel(qkv_ref, o_ref, slab_ref, qr_ref, kr_ref, vt_ref, bias_ref, sa_ref, sb_ref,
                    y_ref, lse_ref):
    nh = N_ATTN_HEADS
    w = ATTN_WIDTH
    seq = qkv_ref.shape[1]
    step = pl.program_id(1)
    hm_b = _head_masks(w, BF16)

    @pl.when((pl.program_id(0) == 0) & (step == 0))
    def _():
        c = lax.broadcasted_iota(jnp.int32, (2 * BAND, nh * BAND), 0)
        col = lax.broadcasted_iota(jnp.int32, (2 * BAND, nh * BAND), 1)
        a = col % BAND
        head = col // BAND
        slope = jnp.full(col.shape, SLOPES_A[-1], F32)
        for h in range(nh - 2, -1, -1):
            slope = jnp.where(head == h, SLOPES_A[h], slope)
        for p, (_, dil) in enumerate(DILATED_PATTERNS):
            for later, offset in enumerate((a - c, a + BAND - c)):
                valid = (offset >= 0) & (offset <= BAND)
                bias_ref[p, later] = jnp.where(valid, -(slope * float(dil)) * offset.astype(F32), NEG_INF)

    @pl.when(step == 0)
    def _():
        def copy(i, carry):
            rows = pl.ds(pl.multiple_of(i * COPY_ROWS, COPY_ROWS), COPY_ROWS)
            x = qkv_ref[0, rows, :].astype(F32)
            for j in range(N_SLABS):
                slab_ref[j, rows, :] = x[:, j * LANES:(j + 1) * LANES]
            return carry
        lax.fori_loop(0, seq // COPY_ROWS, copy, 0)

    def run(p, dil, res, first_pattern):
        nblk = seq // dil // BAND

        def sub_rows(blk):
            if dil == 1:
                return pl.ds(pl.multiple_of(blk * BAND, BAND), BAND)
            return pl.ds(res + dil * BAND * blk, BAND, stride=dil)

        def gather(blk, carry):
            parts = [slab_ref[j, sub_rows(blk), :] for j in range(N_SLABS)]
            dst = pl.ds(pl.multiple_of(blk * BAND, BAND), BAND)
            qr_ref[dst, :] = jnp.concatenate(parts[0:2], axis=1).astype(BF16)
            kr_ref[dst, :] = jnp.concatenate(parts[2:4], axis=1).astype(BF16)
            vt_ref[blk] = jnp.concatenate(parts[4:6], axis=1).T.astype(BF16)
            return carry
        lax.fori_loop(0, nblk, gather, 0)

        def scores(m, dst_ref):
            mm = jnp.minimum(m, nblk - 1)
            q = qr_ref[pl.ds(pl.multiple_of(mm * BAND, BAND), BAND), :]
            q4 = jnp.concatenate([q * hm_b[h] for h in range(nh)], axis=0)
            k0 = pl.multiple_of(jnp.maximum(mm - 1, 0) * BAND, BAND)
            dst_ref[...] = _dot_nt(kr_ref[pl.ds(k0, 2 * BAND), :], q4)

        def finish(m, src_ref):
            later = jnp.minimum(m, 1)
            kb = jnp.maximum(m - 1, 0)
            outs, lses = [], []
            for h in range(nh):
                cs = slice(h * BAND, (h + 1) * BAND)
                hr = slice(h * HEAD_DIM, (h + 1) * HEAD_DIM)
                s = src_ref[:, cs] + bias_ref[p, later, :, cs]
                mx = jnp.max(s, axis=0, keepdims=True)
                pexp = jnp.exp(s - mx)
                l = jnp.sum(pexp, axis=0, keepdims=True)
                vt = jnp.concatenate([vt_ref[kb, hr, :], vt_ref[kb + 1, hr, :]], axis=1)
                outs.append(_dot(vt, pexp.astype(BF16)) * (1.0 / l))
                lses.append(jnp.broadcast_to(mx + jnp.log(l), (HEAD_DIM, BAND)))
            out = jnp.concatenate(outs, axis=0).T
            lse = jnp.concatenate(lses, axis=0).T
            rows = sub_rows(m)
            for sl in range(w // LANES):
                ls = slice(sl * LANES, (sl + 1) * LANES)
                if first_pattern:
                    y_ref[sl, rows, :] = out[:, ls]
                    lse_ref[sl, rows, :] = lse[:, ls]
                else:
                    l_old = lse_ref[sl, rows, :]
                    mx2 = jnp.maximum(l_old, lse[:, ls])
                    e_old = jnp.exp(l_old - mx2)
                    e_new = jnp.exp(lse[:, ls] - mx2)
                    den = e_old + e_new
                    y_ref[sl, rows, :] = (y_ref[sl, rows, :] * e_old + out[:, ls] * e_new) / den
                    lse_ref[sl, rows, :] = mx2 + jnp.log(den)

        scores(0, sa_ref)

        def pair(i, carry):
            m = 2 * i
            scores(m + 1, sb_ref)
            finish(m, sa_ref)
            scores(m + 2, sa_ref)
            finish(m + 1, sb_ref)
            return carry
        lax.fori_loop(0, nblk // 2, pair, 0)

    first_step = 0
    for p, (_, dil) in enumerate(DILATED_PATTERNS):
        @pl.when((step >= first_step) & (step < first_step + dil))
        def _(p=p, dil=dil, first_step=first_step):
            run(p, dil, step - first_step, p == 0)
        first_step += dil

    @pl.when(step == N_DIL_STEPS - 1)
    def _():
        def emit(i, carry):
            rows = pl.ds(pl.multiple_of(i * COPY_ROWS, COPY_ROWS), COPY_ROWS)
            o_ref[0, rows, :] = jnp.concatenate(
                [y_ref[sl, rows, :] for sl in range(w // LANES)], axis=1).astype(BF16)
            return carry
        lax.fori_loop(0, seq // COPY_ROWS, emit, 0)


def _dilated(qkv):
    b, s, _ = qkv.shape
    w = ATTN_WIDTH
    cols = N_ATTN_HEADS * BAND
    return pl.pallas_call(
        _dilated_kernel,
        grid=(b, N_DIL_STEPS),
        in_specs=[pl.BlockSpec((1, s, 3 * w), lambda bi, st: (bi, 0, 0))],
        out_specs=pl.BlockSpec((1, s, w), lambda bi, st: (bi, 0, 0)),
        out_shape=jax.ShapeDtypeStruct((b, s, w), BF16),
        scratch_shapes=[pltpu.VMEM((N_SLABS, s, LANES), F32),
                        pltpu.VMEM((s, w), BF16),
                        pltpu.VMEM((s, w), BF16),
                        pltpu.VMEM((s // BAND, w, BAND), BF16),
                        pltpu.VMEM((len(DILATED_PATTERNS), 2, 2 * BAND, cols), F32),
                        pltpu.VMEM((2 * BAND, cols), F32),
                        pltpu.VMEM((2 * BAND, cols), F32),
                        pltpu.VMEM((w // LANES, s, LANES), F32),
                        pltpu.VMEM((w // LANES, s, LANES), F32)],
        compiler_params=pltpu.CompilerParams(
            dimension_semantics=("arbitrary", "arbitrary"), vmem_limit_bytes=VMEM_LIMIT),
        name="dilated",
    )(qkv)


def _softplus(x):
    return jnp.maximum(x, 0.0) + jnp.log1p(jnp.exp(-jnp.abs(x)))


def _silu(x):
    return x / (1.0 + jnp.exp(-x))


def _ssd_kernel(xbc_ref, z_ref, dt_ref, cw_ref, cb_ref, dtb_ref, alog_ref, dsk_ref, nw_ref,
                y_ref, ext_ref, tail_ref, state_ref, *, chunks):
    rows = chunks * SSM_CHUNK
    halo = 8
    cl = SSM_CHUNK
    gw = D_SSM // SSM_GROUPS
    hpg = SSM_HEADS // SSM_GROUPS

    @pl.when(pl.program_id(1) == 0)
    def _():
        tail_ref[...] = jnp.zeros_like(tail_ref)
        state_ref[...] = jnp.zeros_like(state_ref)

    ext_ref[0:halo, :] = tail_ref[...]
    ext_ref[halo:halo + rows, :] = xbc_ref[0]
    tail_ref[...] = xbc_ref[0, rows - halo:rows, :]
    conv = cb_ref[...] + cw_ref[SSM_CONV - 1:SSM_CONV, :] * ext_ref[halo:halo + rows, :]
    for j in range(1, SSM_CONV):
        conv = conv + cw_ref[SSM_CONV - 1 - j:SSM_CONV - j, :] * ext_ref[halo - j:halo - j + rows, :]
    xact = _silu(conv)

    dt = _softplus(dt_ref[0] + dtb_ref[...])
    da = dt * (-jnp.exp(alog_ref[...]))

    ri = lax.broadcasted_iota(jnp.int32, (cl, cl), 0)
    ci = lax.broadcasted_iota(jnp.int32, (cl, cl), 1)
    tri = ri >= ci
    tril = tri.astype(BF16)
    eh = lax.broadcasted_iota(jnp.int32, (DT_PAD, D_SSM), 0)
    el = lax.broadcasted_iota(jnp.int32, (DT_PAD, D_SSM), 1) // HEAD_DIM
    expand = (eh == el).astype(BF16)
    hm_g = _head_masks(gw, F32)

    for c in range(chunks):
        r0 = c * cl
        xs = xact[r0:r0 + cl, 0:D_SSM]
        bm = xact[r0:r0 + cl, D_SSM:D_SSM + SSM_GROUPS * SSM_STATE]
        cm = xact[r0:r0 + cl, D_SSM + SSM_GROUPS * SSM_STATE:]
        acs = _dot_f32_rhs(tril, da[r0:r0 + cl])
        acs_t = acs.T
        acs_full = _dot_f32_lhs(acs, expand)
        dt_full = _dot_f32_lhs(dt[r0:r0 + cl], expand)
        xd = xs * dt_full
        last = acs_full[cl - 1:cl, :]
        xdd = xd * jnp.exp(last - acs_full)
        exp_acs = jnp.exp(acs_full)
        chunk_decay = jnp.exp(last)
        y_parts = []
        for g in range(SSM_GROUPS):
            gs = slice(g * gw, (g + 1) * gw)
            bg = bm[:, g * SSM_STATE:(g + 1) * SSM_STATE].astype(BF16)
            cg = cm[:, g * SSM_STATE:(g + 1) * SSM_STATE].astype(BF16)
            cb = _dot_nt(cg, bg)
            s_in = state_ref[:, gs]
            y_g = _dot(cg, s_in.astype(BF16)) * exp_acs[:, gs]
            state_ref[:, gs] = s_in * chunk_decay[:, gs] + _dot_tn(bg, xdd[:, gs].astype(BF16))
            xd_g = xd[:, gs]
            for hh in range(hpg):
                h = g * hpg + hh
                diff = acs[:, h:h + 1] - acs_t[h:h + 1, :]
                lmat = jnp.where(tri, jnp.exp(jnp.where(tri, diff, 0.0)), 0.0)
                y_g = y_g + _dot((cb * lmat).astype(BF16), (xd_g * hm_g[hh]).astype(BF16))
            y_parts.append(y_g)
        y = jnp.concatenate(y_parts, axis=1) + dsk_ref[...] * xs
        y = y * _silu(z_ref[0, r0:r0 + cl, :])
        outs = []
        for g in range(SSM_GROUPS):
            yg = y[:, g * gw:(g + 1) * gw]
            ms = jnp.mean(yg * yg, axis=-1, keepdims=True)
            outs.append(yg * lax.rsqrt(ms + EPS) * nw_ref[:, g * gw:(g + 1) * gw])
        y_ref[0, r0:r0 + cl, :] = jnp.concatenate(outs, axis=1).astype(BF16)


def _ssd(xbc, z, dt, conv_w, conv_b, dt_bias, a_log, d_skip_full, norm_w, *, chunks=4):
    b, s, _ = xbc.shape
    rows = chunks * SSM_CHUNK
    blk = lambda width: pl.BlockSpec((1, rows, width), lambda bi, ci: (bi, ci, 0))
    const = lambda shape: pl.BlockSpec(shape, lambda bi, ci: (0, 0))
    return pl.pallas_call(
        functools.partial(_ssd_kernel, chunks=chunks),
        grid=(b, s // rows),
        in_specs=[blk(CONV_DIM), blk(D_SSM), blk(DT_PAD),
                  const((SSM_CONV, CONV_DIM)), const((1, CONV_DIM)), const((1, DT_PAD)),
                  const((1, DT_PAD)), const((1, D_SSM)), const((1, D_SSM))],
        out_specs=blk(D_SSM),
        out_shape=jax.ShapeDtypeStruct((b, s, D_SSM), BF16),
        scratch_shapes=[pltpu.VMEM((rows + 8, CONV_DIM), F32),
                        pltpu.VMEM((8, CONV_DIM), F32),
                        pltpu.VMEM((SSM_STATE, D_SSM), F32)],
        compiler_params=pltpu.CompilerParams(
            dimension_semantics=("parallel", "arbitrary"), vmem_limit_bytes=VMEM_LIMIT),
        name="ssd",
    )(xbc, z, dt, conv_w, conv_b, dt_bias, a_log, d_skip_full, norm_w)


MOBA_QT = LANES


def _moba_kernel(q_ref, k_ref, v_ref, o_ref, kmean_ref, vt_ref, rel_ref, relown_ref, q4t_ref,
                 bias_ref, sa_ref, sb_ref, acc_ref, m_ref, l_ref, *, n_blk):
    nh = N_ATTN_HEADS
    bs = MOBA_BLOCK
    cols = nh * bs
    qb = pl.program_id(1)

    col = lax.broadcasted_iota(jnp.int32, (1, cols), 1)
    col_head = col // bs
    slope = jnp.full((1, cols), SLOPES_C[-1], F32)
    for h in range(nh - 2, -1, -1):
        slope = jnp.where(col_head == h, SLOPES_C[h], slope)

    @pl.when(qb == 0)
    def _():
        for n in range(n_blk):
            kblk = k_ref[0, n * bs:(n + 1) * bs, :].astype(F32)
            kmean_ref[n:n + 1, :] = jnp.sum(kblk, axis=0, keepdims=True) * (1.0 / bs)
            vt_ref[n] = v_ref[0, n * bs:(n + 1) * bs, :].astype(F32).T.astype(BF16)
        c = lax.broadcasted_iota(jnp.int32, (bs, cols), 0)
        a = lax.broadcasted_iota(jnp.int32, (bs, cols), 1) % bs
        rel = -slope * (a - c).astype(F32)
        rel_ref[...] = rel
        relown_ref[...] = jnp.where(a >= c, rel, NEG_INF)

    qt = q_ref[0].astype(F32).T
    frow = lax.broadcasted_iota(jnp.int32, (ATTN_WIDTH, bs), 0) // HEAD_DIM
    for h in range(nh):
        q4t_ref[:, h * bs:(h + 1) * bs] = jnp.where(frow == h, qt, 0.0).astype(BF16)

    km = kmean_ref[...]
    km_hi = km.astype(BF16)
    km_lo = (km - km_hi.astype(F32)).astype(BF16)
    q4t = q4t_ref[...]
    gate = _dot(km_hi, q4t) + _dot(km_lo, q4t)
    blk = lax.broadcasted_iota(jnp.int32, (n_blk, cols), 0)
    blk_f = blk.astype(F32)
    past = blk < qb
    g = jnp.where(past, gate, NEG_INF)
    sel = jnp.zeros((n_blk, cols), jnp.bool_)
    for _ in range(MOBA_TOPK):
        mx = jnp.max(g, axis=0, keepdims=True)
        first = jnp.min(jnp.where(g == mx, blk_f, float(n_blk)), axis=0, keepdims=True)
        pick = blk_f == first
        sel = sel | pick
        g = jnp.where(pick, -jnp.inf, g)
    sel = sel & past
    bias_ref[...] = jnp.where(sel, -(slope * float(bs)) * (qb - blk).astype(F32), NEG_INF)

    def scores(kb, dst_ref):
        kblk = k_ref[0, pl.ds(pl.multiple_of(kb * bs, bs), bs), :]
        for h in range(nh):
            hs = slice(h * bs, (h + 1) * bs)
            dst_ref[:, hs] = _dot(kblk, q4t_ref[:, hs])

    def attend(kb, src_ref, rel_mat_ref, bias_row):
        first = bias_row is None
        for h in range(nh):
            hs = slice(h * bs, (h + 1) * bs)
            ps, alphas = [], []
            for t in range(h * bs // MOBA_QT, (h + 1) * bs // MOBA_QT):
                cs = slice(t * MOBA_QT, (t + 1) * MOBA_QT)
                s = src_ref[:, cs] + rel_mat_ref[:, cs]
                mx = jnp.max(s, axis=0, keepdims=True)
                if first:
                    m_new = shift = mx
                else:
                    b = bias_row[:, cs]
                    m_old = m_ref[t:t + 1, :]
                    m_new = jnp.maximum(m_old, mx + b)
                    shift = m_new - b
                    alphas.append(jnp.exp(m_old - m_new))
                p = jnp.exp(s - shift)
                lsum = jnp.sum(p, axis=0, keepdims=True)
                l_ref[t:t + 1, :] = lsum if first else alphas[-1] * l_ref[t:t + 1, :] + lsum
                m_ref[t:t + 1, :] = m_new
                ps.append(p.astype(BF16))
            pv = _dot(vt_ref[kb, h * HEAD_DIM:(h + 1) * HEAD_DIM, :], jnp.concatenate(ps, axis=1))
            acc_ref[:, hs] = pv if first else jnp.concatenate(alphas, axis=1) * acc_ref[:, hs] + pv

    scores(qb, sa_ref)
    scores(0, sb_ref)
    attend(qb, sa_ref, relown_ref, None)

    def body(i, carry):
        kb = i - 1
        bias_row = bias_ref[pl.ds(kb, 1), :]

        @pl.when(i % 2 == 1)
        def _():
            scores(i, sa_ref)
            attend(kb, sb_ref, rel_ref, bias_row)

        @pl.when(i % 2 == 0)
        def _():
            scores(i, sb_ref)
            attend(kb, sa_ref, rel_ref, bias_row)

        return carry

    lax.fori_loop(1, qb + 1, body, 0)

    inv = 1.0 / l_ref[...]
    heads = []
    for h in range(nh):
        tiles = range(h * bs // MOBA_QT, (h + 1) * bs // MOBA_QT)
        heads.append(jnp.concatenate(
            [acc_ref[:, t * MOBA_QT:(t + 1) * MOBA_QT] * inv[t:t + 1, :] for t in tiles], axis=1))
    o_ref[0] = jnp.concatenate(heads, axis=0).T.astype(BF16)


def _moba(qkv):
    b, s, _ = qkv.shape
    w = ATTN_WIDTH
    bs = MOBA_BLOCK
    n_blk = s // bs
    cols = N_ATTN_HEADS * bs
    return pl.pallas_call(
        functools.partial(_moba_kernel, n_blk=n_blk),
        grid=(b, n_blk),
        in_specs=[pl.BlockSpec((1, bs, w), lambda bi, qi: (bi, qi, 0)),
                  pl.BlockSpec((1, s, w), lambda bi, qi: (bi, 0, 1)),
                  pl.BlockSpec((1, s, w), lambda bi, qi: (bi, 0, 2))],
        out_specs=pl.BlockSpec((1, bs, w), lambda bi, qi: (bi, qi, 0)),
        out_shape=jax.ShapeDtypeStruct((b, s, w), BF16),
        scratch_shapes=[pltpu.VMEM((n_blk, w), F32),
                        pltpu.VMEM((n_blk, w, bs), BF16),
                        pltpu.VMEM((bs, cols), F32),
                        pltpu.VMEM((bs, cols), F32),
                        pltpu.VMEM((w, cols), BF16),
                        pltpu.VMEM((n_blk, cols), F32),
                        pltpu.VMEM((bs, cols), F32),
                        pltpu.VMEM((bs, cols), F32),
                        pltpu.VMEM((HEAD_DIM, cols), F32),
                        pltpu.VMEM((cols // MOBA_QT, MOBA_QT), F32),
                        pltpu.VMEM((cols // MOBA_QT, MOBA_QT), F32)],
        compiler_params=pltpu.CompilerParams(
            dimension_semantics=("parallel", "arbitrary"), vmem_limit_bytes=VMEM_LIMIT),
        name="moba",
    )(qkv, qkv, qkv)


def _out_mlp_kernel(x_ref, ya_ref, yb_ref, yc_ref, wo_ref, n2_ref, w1_ref, w2_ref,
                    out_ref, x1_ref, h_ref, acc_ref):
    k = pl.program_id(1)

    @pl.when(k == 0)
    def _():
        y = jnp.concatenate([ya_ref[...], yb_ref[...], yc_ref[...]], axis=1)
        x1 = x_ref[...] + _dot(y, wo_ref[...])
        x1_ref[...] = x1
        ms = jnp.mean(x1 * x1, axis=-1, keepdims=True)
        h_ref[...] = (x1 * lax.rsqrt(ms + EPS) * n2_ref[...]).astype(BF16)
        acc_ref[...] = jnp.zeros_like(acc_ref)

    u = jnp.maximum(_dot(h_ref[...], w1_ref[...]), 0.0)
    acc_ref[...] += _dot((u * u).astype(BF16), w2_ref[...])

    @pl.when(k == pl.num_programs(1) - 1)
    def _():
        out_ref[...] = x1_ref[...] + acc_ref[...]


def _out_mlp(x2d, ya, yb, yc, w_out, n2, w1, w2, *, tm=512, tf=1024):
    t = x2d.shape[0]
    row = lambda width: pl.BlockSpec((tm, width), lambda i, k: (i, 0))
    return pl.pallas_call(
        _out_mlp_kernel,
        grid=(t // tm, D_FF // tf),
        in_specs=[row(D_MODEL), row(ATTN_WIDTH), row(D_SSM), row(ATTN_WIDTH),
                  pl.BlockSpec((D_MODEL, D_MODEL), lambda i, k: (0, 0)),
                  pl.BlockSpec((1, D_MODEL), lambda i, k: (0, 0)),
                  pl.BlockSpec((D_MODEL, tf), lambda i, k: (0, k)),
                  pl.BlockSpec((tf, D_MODEL), lambda i, k: (k, 0))],
        out_specs=row(D_MODEL),
        out_shape=jax.ShapeDtypeStruct((t, D_MODEL), F32),
        scratch_shapes=[pltpu.VMEM((tm, D_MODEL), F32),
                        pltpu.VMEM((tm, D_MODEL), BF16),
                        pltpu.VMEM((tm, D_MODEL), F32)],
        compiler_params=pltpu.CompilerParams(
            dimension_semantics=("parallel", "arbitrary"), vmem_limit_bytes=VMEM_LIMIT),
        name="out_mlp",
    )(x2d, ya, yb, yc, w_out, n2, w1, w2)


def _layer(x2d, bsz, seq, norm1_w, w_in, a_q_norm, a_k_norm, c_q_norm, c_k_norm, conv_w, conv_b,
           dt_bias, a_log, d_skip, ssm_norm_w, w_out, norm2_w, w_mlp_in, w_mlp_out):
    w_pad = jnp.pad(w_in, ((0, 0), (0, DT_PAD - SSM_HEADS))).astype(BF16)
    scale = HEAD_DIM ** -0.5
    gains = jnp.stack([jnp.tile(a_q_norm, N_ATTN_HEADS) * scale, jnp.tile(a_k_norm, N_ATTN_HEADS),
                       jnp.tile(c_q_norm, N_ATTN_HEADS) * scale, jnp.tile(c_k_norm, N_ATTN_HEADS)])
    lane = jnp.arange(ATTN_WIDTH) // HEAD_DIM
    bd = (lane[:, None] == lane[None, :]).astype(BF16)
    pad_h = lambda v: jnp.pad(v, (0, DT_PAD - SSM_HEADS)).reshape(1, DT_PAD)

    qkva, qkvc, z, xbc, dt = _in_proj(x2d, norm1_w.reshape(1, -1), w_pad, gains, bd)

    ya = _dilated(qkva.reshape(bsz, seq, -1))
    yb =_ssd(xbc.reshape(bsz, seq, -1), z.reshape(bsz, seq, -1), dt.reshape(bsz, seq, -1),
              conv_w, conv_b.reshape(1, -1), pad_h(dt_bias), pad_h(a_log),
              jnp.repeat(d_skip, HEAD_DIM).reshape(1, -1), ssm_norm_w.reshape(1, -1))
    yc = _moba(qkvc.reshape(bsz, seq, -1))

    return _out_mlp(x2d, ya.reshape(bsz * seq, -1), yb.reshape(bsz * seq, -1), yc.reshape(bsz * seq, -1),
                    w_out.astype(BF16), norm2_w.reshape(1, -1),
                    w_mlp_in.astype(BF16), w_mlp_out.astype(BF16))


def kernel(x, norm1_w, w_in, a_q_norm, a_k_norm, c_q_norm, c_k_norm, conv_w, conv_b, dt_bias,
           a_log, d_skip, ssm_norm_w, w_out, norm2_w, w_mlp_in, w_mlp_out):
    bsz, seq, d = x.shape
    x2d = x.reshape(bsz * seq, d)
    for i in range(norm1_w.shape[0]):
        x2d = _layer(x2d, bsz, seq, norm1_w[i], w_in[i], a_q_norm[i], a_k_norm[i], c_q_norm[i],
                     c_k_norm[i], conv_w[i], conv_b[i], dt_bias[i], a_log[i], d_skip[i],
                     ssm_norm_w[i], w_out[i], norm2_w[i], w_mlp_in[i], w_mlp_out[i])
    return x2d.reshape(bsz, seq, d)
```

```python
import functools

import jax
import jax.numpy as jnp
from jax import lax
from jax.experimental import pallas as pl
from jax.experimental.pallas import tpu as pltpu

F32 = jnp.float32
BF16 = jnp.bfloat16

D_MODEL = 1024
HEAD_DIM = 64
N_ATTN_HEADS = 4
ATTN_WIDTH = N_ATTN_HEADS * HEAD_DIM
DILATED_PATTERNS = ((128, 1), (512, 4), (2048, 16))
BAND = 128
MOBA_BLOCK = 256
MOBA_TOPK = 3
SSM_HEADS = 8
D_SSM = SSM_HEADS * HEAD_DIM
SSM_STATE = 128
SSM_GROUPS = 2
SSM_CONV = 4
SSM_CHUNK = 128
CONV_DIM = D_SSM + 2 * SSM_GROUPS * SSM_STATE
D_FF = 4 * D_MODEL
D_IN = 2 * 3 * ATTN_WIDTH + D_SSM + CONV_DIM + SSM_HEADS
LANES = 128
DT_PAD = LANES
D_IN_PAD = D_IN - SSM_HEADS + DT_PAD
EPS = 1e-6
NEG_INF = -1e30
LOG2E = 1.4426950408889634
SLOPES_A = tuple(2.0 ** -(2 * h + 1) for h in range(N_ATTN_HEADS))
SLOPES_C = tuple(2.0 ** -(2 * h + 2) for h in range(N_ATTN_HEADS))

VMEM_LIMIT = 56 * 1024 * 1024
ROW_PARTS = 4


def _split3(a):
    hi = a.astype(BF16)
    r1 = a - hi.astype(F32)
    mid = r1.astype(BF16)
    lo = (r1 - mid.astype(F32)).astype(BF16)
    return hi, mid, lo


def _dot(a, b):
    return jnp.dot(a, b, preferred_element_type=F32)


def _dot_nt(a, b):
    return lax.dot_general(a, b, (((1,), (1,)), ((), ())), preferred_element_type=F32)


def _dot_tn(a, b):
    return lax.dot_general(a, b, (((0,), (0,)), ((), ())), preferred_element_type=F32)


def _dot_f32_lhs(a, sel):
    hi, mid, lo = _split3(a)
    return _dot(hi, sel) + _dot(mid, sel) + _dot(lo, sel)


def _dot_f32_rhs(sel, b):
    hi, mid, lo = _split3(b)
    return _dot(sel, hi) + _dot(sel, mid) + _dot(sel, lo)


def _head_masks(width, dtype):
    lane = lax.broadcasted_iota(jnp.int32, (1, width), 1)
    return [(lane // HEAD_DIM == h).astype(dtype) for h in range(width // HEAD_DIM)]


def _row_slopes(rows_per_head, slopes):
    row = lax.broadcasted_iota(jnp.int32, (len(slopes) * rows_per_head, 1), 0) // rows_per_head
    out = jnp.full(row.shape, slopes[-1], F32)
    for h in range(len(slopes) - 2, -1, -1):
        out = jnp.where(row == h, slopes[h], out)
    return out


def _in_proj_kernel(x_ref, n1_ref, w_ref, g_ref, bd_ref,
                    qkva_ref, qkvc_ref, z_ref, xbc_ref, dt_ref, h_ref):
    pr = x_ref.shape[0] // ROW_PARTS
    for s in range(ROW_PARTS):
        rows = slice(s * pr, (s + 1) * pr)
        x = x_ref[rows, :]
        ms = jnp.mean(x * x, axis=-1, keepdims=True)
        h_ref[rows, :] = (x * lax.rsqrt(ms + EPS) * n1_ref[...]).astype(BF16)
    bd = bd_ref[...]

    def seg(lo, width):
        return _dot(h_ref[...], w_ref[:, lo:lo + width])

    def head_norm(y, g):
        ss = _dot((y * y).astype(BF16), bd)
        return y * lax.rsqrt(ss * (1.0 / HEAD_DIM) + EPS) * g

    w = ATTN_WIDTH
    for i, out_ref in enumerate((qkva_ref, qkvc_ref)):
        base = 3 * w * i
        out_ref[:, 0:w] = head_norm(seg(base, w), g_ref[2 * i:2 * i + 1, :]).astype(BF16)
        out_ref[:, w:2 * w] = head_norm(seg(base + w, w), g_ref[2 * i + 1:2 * i + 2, :]).astype(BF16)
        out_ref[:, 2 * w:3 * w] = seg(base + 2 * w, w).astype(BF16)
    z_ref[...] = seg(6 * w, D_SSM)
    xbc_ref[...] = seg(6 * w + D_SSM, CONV_DIM)
    dt_ref[...] = seg(6 * w + D_SSM + CONV_DIM, DT_PAD)


def _in_proj(x2d, n1, w_pad, gains, bd, *, tm=512):
    t = x2d.shape[0]
    row = lambda i: (i, 0)
    resident = lambda shape: pl.BlockSpec(shape, lambda i: (0, 0), pipeline_mode=pl.Buffered(1))
    return pl.pallas_call(
        _in_proj_kernel,
        grid=(t // tm,),
        in_specs=[
            pl.BlockSpec((tm, D_MODEL), row),
            resident((1, D_MODEL)),
            resident((D_MODEL, D_IN_PAD)),
            resident((4, ATTN_WIDTH)),
            resident((ATTN_WIDTH, ATTN_WIDTH)),
        ],
        out_specs=[
            pl.BlockSpec((tm, 3 * ATTN_WIDTH), row),
            pl.BlockSpec((tm, 3 * ATTN_WIDTH), row),
            pl.BlockSpec((tm, D_SSM), row),
            pl.BlockSpec((tm, CONV_DIM), row),
            pl.BlockSpec((tm, DT_PAD), row),
        ],
        out_shape=[
            jax.ShapeDtypeStruct((t, 3 * ATTN_WIDTH), BF16),
            jax.ShapeDtypeStruct((t, 3 * ATTN_WIDTH), BF16),
            jax.ShapeDtypeStruct((t, D_SSM), F32),
            jax.ShapeDtypeStruct((t, CONV_DIM), F32),
            jax.ShapeDtypeStruct((t, DT_PAD), F32),
        ],
        scratch_shapes=[pltpu.VMEM((tm, D_MODEL), BF16)],
        compiler_params=pltpu.CompilerParams(
            dimension_semantics=("parallel",), vmem_limit_bytes=VMEM_LIMIT),
        name="in_proj",
    )(x2d, n1, w_pad, gains, bd)


N_DIL_STEPS = sum(d for _, d in DILATED_PATTERNS)
N_SLABS = 3 * ATTN_WIDTH // LANES
COPY_ROWS = 512


def _dilated_kernel(qkv_ref, o_ref, slab_ref, qr_ref, kr_ref, vt_ref, bias_ref, sa_ref, sb_ref,
                    y_ref, lse_ref):
    nh = N_ATTN_HEADS
    w = ATTN_WIDTH
    seq = qkv_ref.shape[1]
    step = pl.program_id(1)
    hm_b = _head_masks(w, BF16)

    @pl.when((pl.program_id(0) == 0) & (step == 0))
    def _():
        c = lax.broadcasted_iota(jnp.int32, (2 * BAND, nh * BAND), 0)
        col = lax.broadcasted_iota(jnp.int32, (2 * BAND, nh * BAND), 1)
        a = col % BAND
        head = col // BAND
        slope = jnp.full(col.shape, SLOPES_A[-1], F32)
        for h in range(nh - 2, -1, -1):
            slope = jnp.where(head == h, SLOPES_A[h], slope)
        for p, (_, dil) in enumerate(DILATED_PATTERNS):
            for later, offset in enumerate((a - c, a + BAND - c)):
                valid = (offset >= 0) & (offset <= BAND)
                bias_ref[p, later] = jnp.where(valid, -(slope * (LOG2E * dil)) * offset.astype(F32), NEG_INF)

    @pl.when(step == 0)
    def _():
        def copy(i, carry):
            rows = pl.ds(pl.multiple_of(i * COPY_ROWS, COPY_ROWS), COPY_ROWS)
            x = qkv_ref[0, rows, :].astype(F32)
            for j in range(N_SLABS):
                slab_ref[j, rows, :] = x[:, j * LANES:(j + 1) * LANES]
            return carry
        lax.fori_loop(0, seq // COPY_ROWS, copy, 0)

    def run(p, dil, res, first_pattern):
        nblk = seq // dil // BAND

        def sub_rows(blk):
            if dil == 1:
                return pl.ds(pl.multiple_of(blk * BAND, BAND), BAND)
            return pl.ds(res + dil * BAND * blk, BAND, stride=dil)

        def gather(blk, carry):
            parts = [slab_ref[j, sub_rows(blk), :] for j in range(N_SLABS)]
            dst = pl.ds(pl.multiple_of(blk * BAND, BAND), BAND)
            qr_ref[dst, :] = jnp.concatenate(parts[0:2], axis=1).astype(BF16)
            kr_ref[dst, :] = jnp.concatenate(parts[2:4], axis=1).astype(BF16)
            vt_ref[blk] = jnp.concatenate(parts[4:6], axis=1).T.astype(BF16)
            return carry
        lax.fori_loop(0, nblk, gather, 0)

        def scores(m, dst_ref):
            mm = jnp.minimum(m, nblk - 1)
            q = qr_ref[pl.ds(pl.multiple_of(mm * BAND, BAND), BAND), :]
            q4 = jnp.concatenate([q * hm_b[h] for h in range(nh)], axis=0)
            k0 = pl.multiple_of(jnp.maximum(mm - 1, 0) * BAND, BAND)
            dst_ref[...] = _dot_nt(kr_ref[pl.ds(k0, 2 * BAND), :], q4) + bias_ref[p, jnp.minimum(mm, 1)]

        def finish(m, src_ref):
            kb = jnp.maximum(m - 1, 0)
            outs, lses = [], []
            for h in range(nh):
                cs = slice(h * BAND, (h + 1) * BAND)
                hr = slice(h * HEAD_DIM, (h + 1) * HEAD_DIM)
                mx = jnp.max(src_ref[:, cs], axis=0, keepdims=True)
                pexp = jnp.exp2(src_ref[:, cs] - mx)
                l = jnp.sum(pexp, axis=0, keepdims=True)
                vt = jnp.concatenate([vt_ref[kb, hr, :], vt_ref[kb + 1, hr, :]], axis=1)
                outs.append(_dot(vt, pexp.astype(BF16)) * (1.0 / l))
                lses.append(jnp.broadcast_to(mx + jnp.log2(l), (HEAD_DIM, BAND)))
            out = jnp.concatenate(outs, axis=0).T
            lse = jnp.concatenate(lses, axis=0).T
            rows = sub_rows(m)
            for sl in range(w // LANES):
                ls = slice(sl * LANES, (sl + 1) * LANES)
                if first_pattern:
                    y_ref[sl, rows, :] = out[:, ls]
                    lse_ref[sl, rows, :] = lse[:, ls]
                else:
                    l_old = lse_ref[sl, rows, :]
                    mx2 = jnp.maximum(l_old, lse[:, ls])
                    e_old = jnp.exp2(l_old - mx2)
                    e_new = jnp.exp2(lse[:, ls] - mx2)
                    den = e_old + e_new
                    y_ref[sl, rows, :] = (y_ref[sl, rows, :] * e_old + out[:, ls] * e_new) / den
                    lse_ref[sl, rows, :] = mx2 + jnp.log2(den)

        scores(0, sa_ref)

        def pair(i, carry):
            m = 2 * i
            scores(m + 1, sb_ref)
            finish(m, sa_ref)
            scores(m + 2, sa_ref)
            finish(m + 1, sb_ref)
            return carry
        lax.fori_loop(0, nblk // 2, pair, 0)

    first_step = 0
    for p, (_, dil) in enumerate(DILATED_PATTERNS):
        @pl.when((step >= first_step) & (step < first_step + dil))
        def _(p=p, dil=dil, first_step=first_step):
            run(p, dil, step - first_step, p == 0)
        first_step += dil

    @pl.when(step == N_DIL_STEPS - 1)
    def _():
        def emit(i, carry):
            rows = pl.ds(pl.multiple_of(i * COPY_ROWS, COPY_ROWS), COPY_ROWS)
            o_ref[0, rows, :] = jnp.concatenate(
                [y_ref[sl, rows, :] for sl in range(w // LANES)], axis=1).astype(BF16)
            return carry
        lax.fori_loop(0, seq // COPY_ROWS, emit, 0)


def _dilated(qkv):
    b, s, _ = qkv.shape
    w = ATTN_WIDTH
    cols = N_ATTN_HEADS * BAND
    return pl.pallas_call(
        _dilated_kernel,
        grid=(b, N_DIL_STEPS),
        in_specs=[pl.BlockSpec((1, s, 3 * w), lambda bi, st: (bi, 0, 0))],
        out_specs=pl.BlockSpec((1, s, w), lambda bi, st: (bi, 0, 0)),
        out_shape=jax.ShapeDtypeStruct((b, s, w), BF16),
        scratch_shapes=[pltpu.VMEM((N_SLABS, s, LANES), F32),
                        pltpu.VMEM((s, w), BF16),
                        pltpu.VMEM((s, w), BF16),
                        pltpu.VMEM((s // BAND, w, BAND), BF16),
                        pltpu.VMEM((len(DILATED_PATTERNS), 2, 2 * BAND, cols), F32),
                        pltpu.VMEM((2 * BAND, cols), F32),
                        pltpu.VMEM((2 * BAND, cols), F32),
                        pltpu.VMEM((w // LANES, s, LANES), F32),
                        pltpu.VMEM((w // LANES, s, LANES), F32)],
        compiler_params=pltpu.CompilerParams(
            dimension_semantics=("arbitrary", "arbitrary"), vmem_limit_bytes=VMEM_LIMIT),
        name="dilated",
    )(qkv)


def _softplus(x):
    return jnp.maximum(x, 0.0) + jnp.log1p(jnp.exp(-jnp.abs(x)))


def _silu(x):
    return x / (1.0 + jnp.exp(-x))


def _ssd_kernel(xbc_ref, z_ref, dt_ref, cw_ref, cb_ref, dtb_ref, alog_ref, dsk_ref, nw_ref,
                y_ref, ext_ref, tail_ref, state_ref, *, chunks):
    rows = chunks * SSM_CHUNK
    halo = 8
    cl = SSM_CHUNK
    gw = D_SSM // SSM_GROUPS
    hpg = SSM_HEADS // SSM_GROUPS

    @pl.when(pl.program_id(1) == 0)
    def _():
        tail_ref[...] = jnp.zeros_like(tail_ref)
        state_ref[...] = jnp.zeros_like(state_ref)

    ext_ref[0:halo, :] = tail_ref[...]
    ext_ref[halo:halo + rows, :] = xbc_ref[0]
    tail_ref[...] = xbc_ref[0, rows - halo:rows, :]
    conv = cb_ref[...] + cw_ref[SSM_CONV - 1:SSM_CONV, :] * ext_ref[halo:halo + rows, :]
    for j in range(1, SSM_CONV):
        conv = conv + cw_ref[SSM_CONV - 1 - j:SSM_CONV - j, :] * ext_ref[halo - j:halo - j + rows, :]
    xact = _silu(conv)

    dt = _softplus(dt_ref[0] + dtb_ref[...])
    da = dt * (-jnp.exp(alog_ref[...]))

    ri = lax.broadcasted_iota(jnp.int32, (cl, cl), 0)
    ci = lax.broadcasted_iota(jnp.int32, (cl, cl), 1)
    tri = ri >= ci
    tril = tri.astype(BF16)
    eh = lax.broadcasted_iota(jnp.int32, (DT_PAD, D_SSM), 0)
    el = lax.broadcasted_iota(jnp.int32, (DT_PAD, D_SSM), 1) // HEAD_DIM
    expand = (eh == el).astype(BF16)
    hm_g = _head_masks(gw, F32)

    for c in range(chunks):
        r0 = c * cl
        xs = xact[r0:r0 + cl, 0:D_SSM]
        bm = xact[r0:r0 + cl, D_SSM:D_SSM + SSM_GROUPS * SSM_STATE]
        cm = xact[r0:r0 + cl, D_SSM + SSM_GROUPS * SSM_STATE:]
        acs = _dot_f32_rhs(tril, da[r0:r0 + cl])
        acs_t = acs.T
        acs_full = _dot_f32_lhs(acs, expand)
        dt_full = _dot_f32_lhs(dt[r0:r0 + cl], expand)
        xd = xs * dt_full
        last = acs_full[cl - 1:cl, :]
        xdd = xd * jnp.exp(last - acs_full)
        exp_acs = jnp.exp(acs_full)
        chunk_decay = jnp.exp(last)
        y_parts = []
        for g in range(SSM_GROUPS):
            gs = slice(g * gw, (g + 1) * gw)
            bg = bm[:, g * SSM_STATE:(g + 1) * SSM_STATE].astype(BF16)
            cg = cm[:, g * SSM_STATE:(g + 1) * SSM_STATE].astype(BF16)
            cb = _dot_nt(cg, bg)
            s_in = state_ref[:, gs]
            y_g = _dot(cg, s_in.astype(BF16)) * exp_acs[:, gs]
            state_ref[:, gs] = s_in * chunk_decay[:, gs] + _dot_tn(bg, xdd[:, gs].astype(BF16))
            xd_g = xd[:, gs]
            for hh in range(hpg):
                h = g * hpg + hh
                diff = acs[:, h:h + 1] - acs_t[h:h + 1, :]
                lmat = jnp.where(tri, jnp.exp(jnp.where(tri, diff, 0.0)), 0.0)
                y_g = y_g + _dot((cb * lmat).astype(BF16), (xd_g * hm_g[hh]).astype(BF16))
            y_parts.append(y_g)
        y = jnp.concatenate(y_parts, axis=1) + dsk_ref[...] * xs
        y = y * _silu(z_ref[0, r0:r0 + cl, :])
        outs = []
        for g in range(SSM_GROUPS):
            yg = y[:, g * gw:(g + 1) * gw]
            ms = jnp.mean(yg * yg, axis=-1, keepdims=True)
            outs.append(yg * lax.rsqrt(ms + EPS) * nw_ref[:, g * gw:(g + 1) * gw])
        y_ref[0, r0:r0 + cl, :] = jnp.concatenate(outs, axis=1).astype(BF16)


def _ssd(xbc, z, dt, conv_w, conv_b, dt_bias, a_log, d_skip_full, norm_w, *, chunks=4):
    b, s, _ = xbc.shape
    rows = chunks * SSM_CHUNK
    blk = lambda width: pl.BlockSpec((1, rows, width), lambda bi, ci: (bi, ci, 0))
    const = lambda shape: pl.BlockSpec(shape, lambda bi, ci: (0, 0))
    return pl.pallas_call(
        functools.partial(_ssd_kernel, chunks=chunks),
        grid=(b, s // rows),
        in_specs=[blk(CONV_DIM), blk(D_SSM), blk(DT_PAD),
                  const((SSM_CONV, CONV_DIM)), const((1, CONV_DIM)), const((1, DT_PAD)),
                  const((1, DT_PAD)), const((1, D_SSM)), const((1, D_SSM))],
        out_specs=blk(D_SSM),
        out_shape=jax.ShapeDtypeStruct((b, s, D_SSM), BF16),
        scratch_shapes=[pltpu.VMEM((rows + 8, CONV_DIM), F32),
                        pltpu.VMEM((8, CONV_DIM), F32),
                        pltpu.VMEM((SSM_STATE, D_SSM), F32)],
        compiler_params=pltpu.CompilerParams(
            dimension_semantics=("parallel", "arbitrary"), vmem_limit_bytes=VMEM_LIMIT),
        name="ssd",
    )(xbc, z, dt, conv_w, conv_b, dt_bias, a_log, d_skip_full, norm_w)


MOBA_QT = LANES


def _moba_kernel(q_ref, k_ref, v_ref, o_ref, kmean_ref, vt_ref, rel_ref, relown_ref, q4t_ref,
                 bias_ref, so_ref, sa_ref, sb_ref, acc_ref, m_ref, l_ref, *, n_blk):
    nh = N_ATTN_HEADS
    bs = MOBA_BLOCK
    cols = nh * bs
    qb = pl.program_id(1)

    col = lax.broadcasted_iota(jnp.int32, (1, cols), 1)
    col_head = col // bs
    slope = jnp.full((1, cols), SLOPES_C[-1], F32)
    for h in range(nh - 2, -1, -1):
        slope = jnp.where(col_head == h, SLOPES_C[h], slope)

    @pl.when(qb == 0)
    def _():
        for n in range(n_blk):
            kblk = k_ref[0, n * bs:(n + 1) * bs, :].astype(F32)
            kmean_ref[n:n + 1, :] = jnp.sum(kblk, axis=0, keepdims=True) * (1.0 / bs)
            vt_ref[n] = v_ref[0, n * bs:(n + 1) * bs, :].astype(F32).T.astype(BF16)
        c = lax.broadcasted_iota(jnp.int32, (bs, cols), 0)
        a = lax.broadcasted_iota(jnp.int32, (bs, cols), 1) % bs
        rel = -(slope * LOG2E) * (a - c).astype(F32)
        rel_ref[...] = rel
        relown_ref[...] = jnp.where(a >= c, rel, NEG_INF)

    qt = q_ref[0].astype(F32).T
    frow = lax.broadcasted_iota(jnp.int32, (ATTN_WIDTH, bs), 0) // HEAD_DIM
    for h in range(nh):
        q4t_ref[:, h * bs:(h + 1) * bs] = jnp.where(frow == h, qt, 0.0).astype(BF16)

    km = kmean_ref[...]
    km_hi = km.astype(BF16)
    km_lo = (km - km_hi.astype(F32)).astype(BF16)
    q4t = q4t_ref[...]
    gate = _dot(km_hi, q4t) + _dot(km_lo, q4t)
    blk = lax.broadcasted_iota(jnp.int32, (n_blk, cols), 0)
    blk_f = blk.astype(F32)
    past = blk < qb
    g = jnp.where(past, gate, NEG_INF)
    sel = jnp.zeros((n_blk, cols), jnp.bool_)
    for _ in range(MOBA_TOPK):
        mx = jnp.max(g, axis=0, keepdims=True)
        first = jnp.min(jnp.where(g == mx, blk_f, float(n_blk)), axis=0, keepdims=True)
        pick = blk_f == first
        sel = sel | pick
        g = jnp.where(pick, -jnp.inf, g)
    sel = sel & past
    bias_ref[...] = jnp.where(sel, -(slope * (LOG2E * bs)) * (qb - blk).astype(F32), NEG_INF)

    def scores(kb, dst_ref, slot, rel_mat_ref):
        kb = jnp.minimum(kb, n_blk - 1)
        kblk = k_ref[0, pl.ds(pl.multiple_of(kb * bs, bs), bs), :]
        for h in range(nh):
            hs = slice(h * bs, (h + 1) * bs)
            dst_ref[slot, :, hs] = _dot(kblk, q4t_ref[:, hs]) + rel_mat_ref[:, hs]

    def attend(kb, src_ref, slot, bias_row):
        first = bias_row is None
        for h in range(nh):
            hs = slice(h * bs, (h + 1) * bs)
            ps, alphas = [], []
            for t in range(h * bs // MOBA_QT, (h + 1) * bs // MOBA_QT):
                cs = slice(t * MOBA_QT, (t + 1) * MOBA_QT)
                mx = jnp.max(src_ref[slot, :, cs], axis=0, keepdims=True)
                s = src_ref[slot, :, cs]
                if first:
                    m_new = shift = mx
                else:
                    b = bias_row[:, cs]
                    m_old = m_ref[t:t + 1, :]
                    m_new = jnp.maximum(m_old, mx + b)
                    shift = m_new - b
                    alphas.append(jnp.exp2(m_old - m_new))
                p = jnp.exp2(s - shift)
                lsum = jnp.sum(p, axis=0, keepdims=True)
                l_ref[t:t + 1, :] = lsum if first else alphas[-1] * l_ref[t:t + 1, :] + lsum
                m_ref[t:t + 1, :] = m_new
                ps.append(p.astype(BF16))
            pv = _dot(vt_ref[kb, h * HEAD_DIM:(h + 1) * HEAD_DIM, :], jnp.concatenate(ps, axis=1))
            acc_ref[:, hs] = pv if first else jnp.concatenate(alphas, axis=1) * acc_ref[:, hs] + pv

    scores(qb, so_ref, 0, relown_ref)
    scores(0, sa_ref, 0, rel_ref)
    scores(1, sa_ref, 1, rel_ref)
    attend(qb, so_ref, 0, None)

    def body(j, carry):
        kb = 2 * j

        def step(cur_ref, nxt_ref):
            scores(kb + 2, nxt_ref, 0, rel_ref)
            scores(kb + 3, nxt_ref, 1, rel_ref)
            attend(kb, cur_ref, 0, bias_ref[pl.ds(kb, 1), :])
            attend(kb + 1, cur_ref, 1, bias_ref[pl.ds(kb + 1, 1), :])

        @pl.when(j % 2 == 0)
        def _():
            step(sa_ref, sb_ref)

        @pl.when(j % 2 == 1)
        def _():
            step(sb_ref, sa_ref)

        return carry

    lax.fori_loop(0, (qb + 1) // 2, body, 0)

    inv = 1.0 / l_ref[...]
    heads = []
    for h in range(nh):
        tiles = range(h * bs // MOBA_QT, (h + 1) * bs // MOBA_QT)
        heads.append(jnp.concatenate(
            [acc_ref[:, t * MOBA_QT:(t + 1) * MOBA_QT] * inv[t:t + 1, :] for t in tiles], axis=1))
    o_ref[0] = jnp.concatenate(heads, axis=0).T.astype(BF16)


def _moba(qkv):
    b, s, _ = qkv.shape
    w = ATTN_WIDTH
    bs = MOBA_BLOCK
    n_blk = s // bs
    cols = N_ATTN_HEADS * bs
    return pl.pallas_call(
        functools.partial(_moba_kernel, n_blk=n_blk),
        grid=(b, n_blk),
        in_specs=[pl.BlockSpec((1, bs, w), lambda bi, qi: (bi, qi, 0)),
                  pl.BlockSpec((1, s, w), lambda bi, qi: (bi, 0, 1)),
                  pl.BlockSpec((1, s, w), lambda bi, qi: (bi, 0, 2))],
        out_specs=pl.BlockSpec((1, bs, w), lambda bi, qi: (bi, qi, 0)),
        out_shape=jax.ShapeDtypeStruct((b, s, w), BF16),
        scratch_shapes=[pltpu.VMEM((n_blk, w), F32),
                        pltpu.VMEM((n_blk, w, bs), BF16),
                        pltpu.VMEM((bs, cols), F32),
                        pltpu.VMEM((bs, cols), F32),
                        pltpu.VMEM((w, cols), BF16),
                        pltpu.VMEM((n_blk, cols), F32),
                        pltpu.VMEM((1, bs, cols), F32),
                        pltpu.VMEM((2, bs, cols), F32),
                        pltpu.VMEM((2, bs, cols), F32),
                        pltpu.VMEM((HEAD_DIM, cols), F32),
                        pltpu.VMEM((cols // MOBA_QT, MOBA_QT), F32),
                        pltpu.VMEM((cols // MOBA_QT, MOBA_QT), F32)],
        compiler_params=pltpu.CompilerParams(
            dimension_semantics=("parallel", "arbitrary"), vmem_limit_bytes=VMEM_LIMIT),
        name="moba",
    )(qkv, qkv, qkv)


def _out_mlp_kernel(x_ref, ya_ref, yb_ref, yc_ref, wo_ref, n2_ref, w1_ref, w2_ref,
                    out_ref, h_ref, *, tf):
    tm = x_ref.shape[0]
    pr = tm // ROW_PARTS
    for s in range(ROW_PARTS):
        rows = slice(s * pr, (s + 1) * pr)
        y = jnp.concatenate([ya_ref[rows, :], yb_ref[rows, :], yc_ref[rows, :]], axis=1)
        x1 = x_ref[rows, :] + _dot(y, wo_ref[...])
        out_ref[rows, :] = x1
        ms = jnp.mean(x1 * x1, axis=-1, keepdims=True)
        h_ref[rows, :] = (x1 * lax.rsqrt(ms + EPS) * n2_ref[...]).astype(BF16)
    for c in range(D_FF // tf):
        cols = slice(c * tf, (c + 1) * tf)
        u = jnp.maximum(_dot(h_ref[...], w1_ref[:, cols]), 0.0)
        out_ref[...] += _dot((u * u).astype(BF16), w2_ref[cols, :])


def _out_mlp(x2d, ya, yb, yc, w_out, n2, w1, w2, *, tm=512, tf=1024):
    t = x2d.shape[0]
    row = lambda width: pl.BlockSpec((tm, width), lambda i: (i, 0))
    resident = lambda shape: pl.BlockSpec(shape, lambda i: (0, 0), pipeline_mode=pl.Buffered(1))
    return pl.pallas_call(
        functools.partial(_out_mlp_kernel, tf=tf),
        grid=(t // tm,),
        in_specs=[row(D_MODEL), row(ATTN_WIDTH), row(D_SSM), row(ATTN_WIDTH),
                  resident((D_MODEL, D_MODEL)), resident((1, D_MODEL)),
                  resident((D_MODEL, D_FF)), resident((D_FF, D_MODEL))],
        out_specs=row(D_MODEL),
        out_shape=jax.ShapeDtypeStruct((t, D_MODEL), F32),
        scratch_shapes=[pltpu.VMEM((tm, D_MODEL), BF16)],
        compiler_params=pltpu.CompilerParams(
            dimension_semantics=("parallel",), vmem_limit_bytes=VMEM_LIMIT),
        name="out_mlp",
    )(x2d, ya, yb, yc, w_out, n2, w1, w2)


def _layer(x2d, bsz, seq, norm1_w, w_in, a_q_norm, a_k_norm, c_q_norm, c_k_norm, conv_w, conv_b,
           dt_bias, a_log, d_skip, ssm_norm_w, w_out, norm2_w, w_mlp_in, w_mlp_out):
    w_pad = jnp.pad(w_in, ((0, 0), (0, DT_PAD - SSM_HEADS))).astype(BF16)
    scale = HEAD_DIM ** -0.5 * LOG2E
    gains = jnp.stack([jnp.tile(a_q_norm, N_ATTN_HEADS) * scale, jnp.tile(a_k_norm, N_ATTN_HEADS),
                       jnp.tile(c_q_norm, N_ATTN_HEADS) * scale, jnp.tile(c_k_norm, N_ATTN_HEADS)])
    lane = jnp.arange(ATTN_WIDTH) // HEAD_DIM
    bd = (lane[:, None] == lane[None, :]).astype(BF16)
    pad_h = lambda v: jnp.pad(v, (0, DT_PAD - SSM_HEADS)).reshape(1, DT_PAD)

    qkva, qkvc, z, xbc, dt = _in_proj(x2d, norm1_w.reshape(1, -1), w_pad, gains, bd)

    ya = _dilated(qkva.reshape(bsz, seq, -1))
    yb =_ssd(xbc.reshape(bsz, seq, -1), z.reshape(bsz, seq, -1), dt.reshape(bsz, seq, -1),
              conv_w, conv_b.reshape(1, -1), pad_h(dt_bias), pad_h(a_log),
              jnp.repeat(d_skip, HEAD_DIM).reshape(1, -1), ssm_norm_w.reshape(1, -1))
    yc = _moba(qkvc.reshape(bsz, seq, -1))

    return _out_mlp(x2d, ya.reshape(bsz * seq, -1), yb.reshape(bsz * seq, -1), yc.reshape(bsz * seq, -1),
                    w_out.astype(BF16), norm2_w.reshape(1, -1),
                    w_mlp_in.astype(BF16), w_mlp_out.astype(BF16))


def kernel(x, norm1_w, w_in, a_q_norm, a_k_norm, c_q_norm, c_k_norm, conv_w, conv_b, dt_bias,
           a_log, d_skip, ssm_norm_w, w_out, norm2_w, w_mlp_in, w_mlp_out):
    bsz, seq, d = x.shape
    x2d = x.reshape(bsz * seq, d)
    for i in range(norm1_w.shape[0]):
        x2d = _layer(x2d, bsz, seq, norm1_w[i], w_in[i], a_q_norm[i], a_k_norm[i], c_q_norm[i],
                     c_k_norm[i], conv_w[i], conv_b[i], dt_bias[i], a_log[i], d_skip[i],
                     ssm_norm_w[i], w_out[i], norm2_w[i], w_mlp_in[i], w_mlp_out[i])
    return x2d.reshape(bsz, seq, d)
```

```python
import functools

import jax
import jax.numpy as jnp
from jax import lax
from jax.experimental import pallas as pl
from jax.experimental.pallas import tpu as pltpu

F32 = jnp.float32
BF16 = jnp.bfloat16

D_MODEL = 1024
HEAD_DIM = 64
N_ATTN_HEADS = 4
ATTN_WIDTH = N_ATTN_HEADS * HEAD_DIM
DILATED_PATTERNS = ((128, 1), (512, 4), (2048, 16))
BAND = 128
MOBA_BLOCK = 256
MOBA_TOPK = 3
SSM_HEADS = 8
D_SSM = SSM_HEADS * HEAD_DIM
SSM_STATE = 128
SSM_GROUPS = 2
SSM_CONV = 4
SSM_CHUNK = 128
CONV_DIM = D_SSM + 2 * SSM_GROUPS * SSM_STATE
D_FF = 4 * D_MODEL
D_IN = 2 * 3 * ATTN_WIDTH + D_SSM + CONV_DIM + SSM_HEADS
LANES = 128
DT_PAD = LANES
D_IN_PAD = D_IN - SSM_HEADS + DT_PAD
EPS = 1e-6
NEG_INF = -1e30
LOG2E = 1.4426950408889634
SLOPES_A = tuple(2.0 ** -(2 * h + 1) for h in range(N_ATTN_HEADS))
SLOPES_C = tuple(2.0 ** -(2 * h + 2) for h in range(N_ATTN_HEADS))

VMEM_LIMIT = 56 * 1024 * 1024
ROW_PARTS = 4


def _split3(a):
    hi = a.astype(BF16)
    r1 = a - hi.astype(F32)
    mid = r1.astype(BF16)
    lo = (r1 - mid.astype(F32)).astype(BF16)
    return hi, mid, lo


def _dot(a, b):
    return jnp.dot(a, b, preferred_element_type=F32)


def _dot_nt(a, b):
    return lax.dot_general(a, b, (((1,), (1,)), ((), ())), preferred_element_type=F32)


def _dot_tn(a, b):
    return lax.dot_general(a, b, (((0,), (0,)), ((), ())), preferred_element_type=F32)


def _dot_f32_lhs(a, sel):
    hi, mid, lo = _split3(a)
    return _dot(hi, sel) + _dot(mid, sel) + _dot(lo, sel)


def _dot_f32_rhs(sel, b):
    hi, mid, lo = _split3(b)
    return _dot(sel, hi) + _dot(sel, mid) + _dot(sel, lo)


def _head_masks(width, dtype):
    lane = lax.broadcasted_iota(jnp.int32, (1, width), 1)
    return [(lane // HEAD_DIM == h).astype(dtype) for h in range(width // HEAD_DIM)]


def _row_slopes(rows_per_head, slopes):
    row = lax.broadcasted_iota(jnp.int32, (len(slopes) * rows_per_head, 1), 0) // rows_per_head
    out = jnp.full(row.shape, slopes[-1], F32)
    for h in range(len(slopes) - 2, -1, -1):
        out = jnp.where(row == h, slopes[h], out)
    return out


def _in_proj_kernel(x_ref, n1_ref, w_ref, g_ref, bd_ref,
                    qkva_ref, qkvc_ref, z_ref, xbc_ref, dt_ref, h_ref):
    pr = x_ref.shape[0] // ROW_PARTS
    for s in range(ROW_PARTS):
        rows = slice(s * pr, (s + 1) * pr)
        x = x_ref[rows, :]
        ms = jnp.mean(x * x, axis=-1, keepdims=True)
        h_ref[rows, :] = (x * lax.rsqrt(ms + EPS) * n1_ref[...]).astype(BF16)
    bd = bd_ref[...]

    def seg(lo, width):
        return _dot(h_ref[...], w_ref[:, lo:lo + width])

    def head_norm(y, g):
        ss = _dot((y * y).astype(BF16), bd)
        return y * lax.rsqrt(ss * (1.0 / HEAD_DIM) + EPS) * g

    w = ATTN_WIDTH
    for i, out_ref in enumerate((qkva_ref, qkvc_ref)):
        base = 3 * w * i
        out_ref[:, 0:w] = head_norm(seg(base, w), g_ref[2 * i:2 * i + 1, :]).astype(BF16)
        out_ref[:, w:2 * w] = head_norm(seg(base + w, w), g_ref[2 * i + 1:2 * i + 2, :]).astype(BF16)
        out_ref[:, 2 * w:3 * w] = seg(base + 2 * w, w).astype(BF16)
    z_ref[...] = seg(6 * w, D_SSM)
    xbc_ref[...] = seg(6 * w + D_SSM, CONV_DIM)
    dt_ref[...] = seg(6 * w + D_SSM + CONV_DIM, DT_PAD)


def _in_proj(x2d, n1, w_pad, gains, bd, *, tm=512):
    t = x2d.shape[0]
    row = lambda i: (i, 0)
    resident = lambda shape: pl.BlockSpec(shape, lambda i: (0, 0), pipeline_mode=pl.Buffered(1))
    return pl.pallas_call(
        _in_proj_kernel,
        grid=(t // tm,),
        in_specs=[
            pl.BlockSpec((tm, D_MODEL), row),
            resident((1, D_MODEL)),
            resident((D_MODEL, D_IN_PAD)),
            resident((4, ATTN_WIDTH)),
            resident((ATTN_WIDTH, ATTN_WIDTH)),
        ],
        out_specs=[
            pl.BlockSpec((tm, 3 * ATTN_WIDTH), row),
            pl.BlockSpec((tm, 3 * ATTN_WIDTH), row),
            pl.BlockSpec((tm, D_SSM), row),
            pl.BlockSpec((tm, CONV_DIM), row),
            pl.BlockSpec((tm, DT_PAD), row),
        ],
        out_shape=[
            jax.ShapeDtypeStruct((t, 3 * ATTN_WIDTH), BF16),
            jax.ShapeDtypeStruct((t, 3 * ATTN_WIDTH), BF16),
            jax.ShapeDtypeStruct((t, D_SSM), F32),
            jax.ShapeDtypeStruct((t, CONV_DIM), F32),
            jax.ShapeDtypeStruct((t, DT_PAD), F32),
        ],
        scratch_shapes=[pltpu.VMEM((tm, D_MODEL), BF16)],
        compiler_params=pltpu.CompilerParams(
            dimension_semantics=("parallel",), vmem_limit_bytes=VMEM_LIMIT),
        name="in_proj",
    )(x2d, n1, w_pad, gains, bd)


N_SLABS = 3 * ATTN_WIDTH // LANES
COPY_ROWS = 512
GATHER_UNROLL = 4


def _dilated_kernel(qkv_ref, o_ref, slab_ref, qr_ref, kr_ref, vt_ref, bias_ref, sa_ref, sb_ref,
                    st_ref, y_ref, lse_ref):
    nh = N_ATTN_HEADS
    w = ATTN_WIDTH
    seq = qkv_ref.shape[1]
    n_tiles = seq // BAND
    hm_b = _head_masks(w, BF16)

    @pl.when(pl.program_id(0) == 0)
    def _():
        c = lax.broadcasted_iota(jnp.int32, (2 * BAND, nh * BAND), 0)
        col = lax.broadcasted_iota(jnp.int32, (2 * BAND, nh * BAND), 1)
        a = col % BAND
        head = col // BAND
        slope = jnp.full(col.shape, SLOPES_A[-1], F32)
        for h in range(nh - 2, -1, -1):
            slope = jnp.where(head == h, SLOPES_A[h], slope)
        for p, (_, dil) in enumerate(DILATED_PATTERNS):
            for later, offset in enumerate((a - c, a + BAND - c)):
                valid = (offset >= 0) & (offset <= BAND)
                bias_ref[p, later] = jnp.where(valid, -(slope * (LOG2E * dil)) * offset.astype(F32), NEG_INF)

    def copy(i, carry):
        rows = pl.ds(pl.multiple_of(i * COPY_ROWS, COPY_ROWS), COPY_ROWS)
        x = qkv_ref[0, rows, :].astype(F32)
        for j in range(N_SLABS):
            slab_ref[j, rows, :] = x[:, j * LANES:(j + 1) * LANES]
        return carry
    lax.fori_loop(0, seq // COPY_ROWS, copy, 0)

    def run(p, dil, first_pattern):
        nblk = seq // dil // BAND

        def sub_rows(t):
            if dil == 1:
                return pl.ds(pl.multiple_of(t * BAND, BAND), BAND)
            return pl.ds(t // nblk + dil * BAND * (t % nblk), BAND, stride=dil)

        def gather(i, carry):
            for u in range(GATHER_UNROLL):
                t = i * GATHER_UNROLL + u
                parts = [slab_ref[j, sub_rows(t), :] for j in range(N_SLABS)]
                dst = pl.ds(pl.multiple_of(t * BAND, BAND), BAND)
                qr_ref[dst, :] = jnp.concatenate(parts[0:2], axis=1).astype(BF16)
                kr_ref[dst, :] = jnp.concatenate(parts[2:4], axis=1).astype(BF16)
                vt_ref[t] = jnp.concatenate(parts[4:6], axis=1).T.astype(BF16)
            return carry
        lax.fori_loop(0, n_tiles // GATHER_UNROLL, gather, 0)

        def scores(t, dst_ref):
            tt = jnp.minimum(t, n_tiles - 1)
            later = jnp.minimum(tt % nblk, 1)
            q = qr_ref[pl.ds(pl.multiple_of(tt * BAND, BAND), BAND), :]
            q4 = jnp.concatenate([q * hm_b[h] for h in range(nh)], axis=0)
            k0 = pl.multiple_of((tt - later) * BAND, BAND)
            dst_ref[...] = _dot_nt(kr_ref[pl.ds(k0, 2 * BAND), :], q4) + bias_ref[p, later]

        def softmax_pv(t, src_ref, slot):
            kb = t - jnp.minimum(t % nblk, 1)
            outs, lses = [], []
            for h in range(nh):
                cs = slice(h * BAND, (h + 1) * BAND)
                hr = slice(h * HEAD_DIM, (h + 1) * HEAD_DIM)
                mx = jnp.max(src_ref[:, cs], axis=0, keepdims=True)
                pexp = jnp.exp2(src_ref[:, cs] - mx)
                l = jnp.sum(pexp, axis=0, keepdims=True)
                vt = jnp.concatenate([vt_ref[kb, hr, :], vt_ref[kb + 1, hr, :]], axis=1)
                outs.append(_dot(vt, pexp.astype(BF16)) * (1.0 / l))
                lses.append(jnp.broadcast_to(mx + jnp.log2(l), (HEAD_DIM, BAND)))
            st_ref[slot, 0] = jnp.concatenate(outs, axis=0)
            st_ref[slot, 1] = jnp.concatenate(lses, axis=0)

        def merge(m, slot):
            out = st_ref[slot, 0].T
            lse = st_ref[slot, 1].T
            rows = sub_rows(jnp.maximum(m, 0))
            for sl in range(w // LANES):
                ls = slice(sl * LANES, (sl + 1) * LANES)
                if first_pattern:
                    y_ref[sl, rows, :] = out[:, ls]
                    lse_ref[sl, rows, :] = lse[:, ls]
                else:
                    l_old = lse_ref[sl, rows, :]
                    mx2 = jnp.maximum(l_old, lse[:, ls])
                    e_old = jnp.exp2(l_old - mx2)
                    e_new = jnp.exp2(lse[:, ls] - mx2)
                    den = e_old + e_new
                    y_ref[sl, rows, :] = (y_ref[sl, rows, :] * e_old + out[:, ls] * e_new) / den
                    lse_ref[sl, rows, :] = mx2 + jnp.log2(den)

        scores(0, sa_ref)
        st_ref[1, 0] = jnp.zeros(st_ref.shape[2:], F32)
        st_ref[1, 1] = jnp.full(st_ref.shape[2:], NEG_INF, F32)

        def pair(i, carry):
            t = 2 * i
            scores(t + 1, sb_ref)
            softmax_pv(t, sa_ref, 0)
            merge(t - 1, 1)
            scores(t + 2, sa_ref)
            softmax_pv(t + 1, sb_ref, 1)
            merge(t, 0)
            return carry
        lax.fori_loop(0, n_tiles // 2, pair, 0)
        merge(n_tiles - 1, 1)

    for p, (_, dil) in enumerate(DILATED_PATTERNS):
        run(p, dil, p == 0)

    def emit(i, carry):
        rows = pl.ds(pl.multiple_of(i * COPY_ROWS, COPY_ROWS), COPY_ROWS)
        o_ref[0, rows, :] = jnp.concatenate(
            [y_ref[sl, rows, :] for sl in range(w // LANES)], axis=1).astype(BF16)
        return carry
    lax.fori_loop(0, seq // COPY_ROWS, emit, 0)


def _dilated(qkv):
    b, s, _ = qkv.shape
    w = ATTN_WIDTH
    cols = N_ATTN_HEADS * BAND
    return pl.pallas_call(
        _dilated_kernel,
        grid=(b,),
        in_specs=[pl.BlockSpec((1, s, 3 * w), lambda bi: (bi, 0, 0))],
        out_specs=pl.BlockSpec((1, s, w), lambda bi: (bi, 0, 0)),
        out_shape=jax.ShapeDtypeStruct((b, s, w), BF16),
        scratch_shapes=[pltpu.VMEM((N_SLABS, s, LANES), F32),
                        pltpu.VMEM((s, w), BF16),
                        pltpu.VMEM((s, w), BF16),
                        pltpu.VMEM((s // BAND, w, BAND), BF16),
                        pltpu.VMEM((len(DILATED_PATTERNS), 2, 2 * BAND, cols), F32),
                        pltpu.VMEM((2 * BAND, cols), F32),
                        pltpu.VMEM((2 * BAND, cols), F32),
                        pltpu.VMEM((2, 2, w, BAND), F32),
                        pltpu.VMEM((w // LANES, s, LANES), F32),
                        pltpu.VMEM((w // LANES, s, LANES), F32)],
        compiler_params=pltpu.CompilerParams(
            dimension_semantics=("arbitrary",), vmem_limit_bytes=VMEM_LIMIT),
        name="dilated",
    )(qkv)


def _softplus(x):
    return jnp.maximum(x, 0.0) + jnp.log1p(jnp.exp(-jnp.abs(x)))


def _silu(x):
    return x / (1.0 + jnp.exp(-x))


def _ssd_kernel(xbc_ref, z_ref, dt_ref, cw_ref, cb_ref, dtb_ref, alog_ref, dsk_ref, nw_ref,
                y_ref, ext_ref, tail_ref, state_ref, *, chunks):
    rows = chunks * SSM_CHUNK
    halo = 8
    cl = SSM_CHUNK
    gw = D_SSM // SSM_GROUPS
    hpg = SSM_HEADS // SSM_GROUPS

    @pl.when(pl.program_id(1) == 0)
    def _():
        tail_ref[...] = jnp.zeros_like(tail_ref)
        state_ref[...] = jnp.zeros_like(state_ref)

    ext_ref[0:halo, :] = tail_ref[...]
    ext_ref[halo:halo + rows, :] = xbc_ref[0]
    tail_ref[...] = xbc_ref[0, rows - halo:rows, :]
    conv = cb_ref[...] + cw_ref[SSM_CONV - 1:SSM_CONV, :] * ext_ref[halo:halo + rows, :]
    for j in range(1, SSM_CONV):
        conv = conv + cw_ref[SSM_CONV - 1 - j:SSM_CONV - j, :] * ext_ref[halo - j:halo - j + rows, :]
    xact = _silu(conv)

    dt = _softplus(dt_ref[0] + dtb_ref[...])
    da = dt * (-jnp.exp(alog_ref[...]))

    ri = lax.broadcasted_iota(jnp.int32, (cl, cl), 0)
    ci = lax.broadcasted_iota(jnp.int32, (cl, cl), 1)
    tri = ri >= ci
    tril = tri.astype(BF16)
    eh = lax.broadcasted_iota(jnp.int32, (DT_PAD, D_SSM), 0)
    el = lax.broadcasted_iota(jnp.int32, (DT_PAD, D_SSM), 1) // HEAD_DIM
    expand = (eh == el).astype(BF16)
    hm_g = _head_masks(gw, F32)

    for c in range(chunks):
        r0 = c * cl
        xs = xact[r0:r0 + cl, 0:D_SSM]
        bm = xact[r0:r0 + cl, D_SSM:D_SSM + SSM_GROUPS * SSM_STATE]
        cm = xact[r0:r0 + cl, D_SSM + SSM_GROUPS * SSM_STATE:]
        acs = _dot_f32_rhs(tril, da[r0:r0 + cl])
        acs_t = acs.T
        acs_full = _dot_f32_lhs(acs, expand)
        dt_full = _dot_f32_lhs(dt[r0:r0 + cl], expand)
        xd = xs * dt_full
        last = acs_full[cl - 1:cl, :]
        xdd = xd * jnp.exp(last - acs_full)
        exp_acs = jnp.exp(acs_full)
        chunk_decay = jnp.exp(last)
        y_parts = []
        for g in range(SSM_GROUPS):
            gs = slice(g * gw, (g + 1) * gw)
            bg = bm[:, g * SSM_STATE:(g + 1) * SSM_STATE].astype(BF16)
            cg = cm[:, g * SSM_STATE:(g + 1) * SSM_STATE].astype(BF16)
            cb = _dot_nt(cg, bg)
            s_in = state_ref[:, gs]
            y_g = _dot(cg, s_in.astype(BF16)) * exp_acs[:, gs]
            state_ref[:, gs] = s_in * chunk_decay[:, gs] + _dot_tn(bg, xdd[:, gs].astype(BF16))
            xd_g = xd[:, gs]
            for hh in range(hpg):
                h = g * hpg + hh
                diff = acs[:, h:h + 1] - acs_t[h:h + 1, :]
                lmat = jnp.where(tri, jnp.exp(jnp.where(tri, diff, 0.0)), 0.0)
                y_g = y_g + _dot((cb * lmat).astype(BF16), (xd_g * hm_g[hh]).astype(BF16))
            y_parts.append(y_g)
        y = jnp.concatenate(y_parts, axis=1) + dsk_ref[...] * xs
        y = y * _silu(z_ref[0, r0:r0 + cl, :])
        outs = []
        for g in range(SSM_GROUPS):
            yg = y[:, g * gw:(g + 1) * gw]
            ms = jnp.mean(yg * yg, axis=-1, keepdims=True)
            outs.append(yg * lax.rsqrt(ms + EPS) * nw_ref[:, g * gw:(g + 1) * gw])
        y_ref[0, r0:r0 + cl, :] = jnp.concatenate(outs, axis=1).astype(BF16)


def _ssd(xbc, z, dt, conv_w, conv_b, dt_bias, a_log, d_skip_full, norm_w, *, chunks=4):
    b, s, _ = xbc.shape
    rows = chunks * SSM_CHUNK
    blk = lambda width: pl.BlockSpec((1, rows, width), lambda bi, ci: (bi, ci, 0))
    const = lambda shape: pl.BlockSpec(shape, lambda bi, ci: (0, 0))
    return pl.pallas_call(
        functools.partial(_ssd_kernel, chunks=chunks),
        grid=(b, s // rows),
        in_specs=[blk(CONV_DIM), blk(D_SSM), blk(DT_PAD),
                  const((SSM_CONV, CONV_DIM)), const((1, CONV_DIM)), const((1, DT_PAD)),
                  const((1, DT_PAD)), const((1, D_SSM)), const((1, D_SSM))],
        out_specs=blk(D_SSM),
        out_shape=jax.ShapeDtypeStruct((b, s, D_SSM), BF16),
        scratch_shapes=[pltpu.VMEM((rows + 8, CONV_DIM), F32),
                        pltpu.VMEM((8, CONV_DIM), F32),
                        pltpu.VMEM((SSM_STATE, D_SSM), F32)],
        compiler_params=pltpu.CompilerParams(
            dimension_semantics=("parallel", "arbitrary"), vmem_limit_bytes=VMEM_LIMIT),
        name="ssd",
    )(xbc, z, dt, conv_w, conv_b, dt_bias, a_log, d_skip_full, norm_w)


MOBA_QT = LANES


def _moba_kernel(q_ref, k_ref, v_ref, o_ref, kmean_ref, vt_ref, rel_ref, relown_ref, q4t_ref,
                 bias_ref, so_ref, sa_ref, sb_ref, acc_ref, m_ref, l_ref, *, n_blk):
    nh = N_ATTN_HEADS
    bs = MOBA_BLOCK
    cols = nh * bs
    qb = pl.program_id(1)

    col = lax.broadcasted_iota(jnp.int32, (1, cols), 1)
    col_head = col // bs
    slope = jnp.full((1, cols), SLOPES_C[-1], F32)
    for h in range(nh - 2, -1, -1):
        slope = jnp.where(col_head == h, SLOPES_C[h], slope)

    @pl.when(qb == 0)
    def _():
        for n in range(n_blk):
            kblk = k_ref[0, n * bs:(n + 1) * bs, :].astype(F32)
            kmean_ref[n:n + 1, :] = jnp.sum(kblk, axis=0, keepdims=True) * (1.0 / bs)
            vt_ref[n] = v_ref[0, n * bs:(n + 1) * bs, :].astype(F32).T.astype(BF16)
        c = lax.broadcasted_iota(jnp.int32, (bs, cols), 0)
        a = lax.broadcasted_iota(jnp.int32, (bs, cols), 1) % bs
        rel = -(slope * LOG2E) * (a - c).astype(F32)
        rel_ref[...] = rel
        relown_ref[...] = jnp.where(a >= c, rel, NEG_INF)

    qt = q_ref[0].astype(F32).T
    frow = lax.broadcasted_iota(jnp.int32, (ATTN_WIDTH, bs), 0) // HEAD_DIM
    for h in range(nh):
        q4t_ref[:, h * bs:(h + 1) * bs] = jnp.where(frow == h, qt, 0.0).astype(BF16)

    def block_gate():
        km = kmean_ref[...]
        km_hi = km.astype(BF16)
        km_lo = (km - km_hi.astype(F32)).astype(BF16)
        q4t = q4t_ref[...]
        return _dot(km_hi, q4t) + _dot(km_lo, q4t)

    def select_blocks(gate):
        blk =lax.broadcasted_iota(jnp.int32, (n_blk, cols), 0)
        blk_f = blk.astype(F32)
        past = blk < qb
        g = jnp.where(past, gate, NEG_INF)
        sel = jnp.zeros((n_blk, cols), jnp.bool_)
        for _ in range(MOBA_TOPK):
            mx = jnp.max(g, axis=0, keepdims=True)
            first = jnp.min(jnp.where(g == mx, blk_f, float(n_blk)), axis=0, keepdims=True)
            pick = blk_f == first
            sel = sel | pick
            g = jnp.where(pick, -jnp.inf, g)
        sel = sel & past
        bias_ref[...] = jnp.where(sel, -(slope * (LOG2E * bs)) * (qb - blk).astype(F32), NEG_INF)

    def scores(kb, dst_ref, slot, rel_mat_ref):
        kb = jnp.minimum(kb, n_blk - 1)
        kblk = k_ref[0, pl.ds(pl.multiple_of(kb * bs, bs), bs), :]
        for h in range(nh):
            hs = slice(h * bs, (h + 1) * bs)
            dst_ref[slot, :, hs] = _dot(kblk, q4t_ref[:, hs]) + rel_mat_ref[:, hs]

    def attend(kb, src_ref, slot, bias_row):
        first = bias_row is None
        for h in range(nh):
            hs = slice(h * bs, (h + 1) * bs)
            ps, alphas = [], []
            for t in range(h * bs // MOBA_QT, (h + 1) * bs // MOBA_QT):
                cs = slice(t * MOBA_QT, (t + 1) * MOBA_QT)
                mx = jnp.max(src_ref[slot, :, cs], axis=0, keepdims=True)
                s = src_ref[slot, :, cs]
                if first:
                    m_new = shift = mx
                else:
                    b = bias_row[:, cs]
                    m_old = m_ref[t:t + 1, :]
                    m_new = jnp.maximum(m_old, mx + b)
                    shift = m_new - b
                    alphas.append(jnp.exp2(m_old - m_new))
                p = jnp.exp2(s - shift)
                lsum = jnp.sum(p, axis=0, keepdims=True)
                l_ref[t:t + 1, :] = lsum if first else alphas[-1] * l_ref[t:t + 1, :] + lsum
                m_ref[t:t + 1, :] = m_new
                ps.append(p.astype(BF16))
            pv = _dot(vt_ref[kb, h * HEAD_DIM:(h + 1) * HEAD_DIM, :], jnp.concatenate(ps, axis=1))
            acc_ref[:, hs] = pv if first else jnp.concatenate(alphas, axis=1) * acc_ref[:, hs] + pv

    gate = block_gate()
    scores(qb, so_ref, 0, relown_ref)
    scores(0, sa_ref, 0, rel_ref)
    scores(1, sa_ref, 1, rel_ref)
    attend(qb, so_ref, 0, None)
    select_blocks(gate)

    def body(j, carry):
        kb = 2 * j

        def step(cur_ref, nxt_ref):
            scores(kb + 2, nxt_ref, 0, rel_ref)
            scores(kb + 3, nxt_ref, 1, rel_ref)
            attend(kb, cur_ref, 0, bias_ref[pl.ds(kb, 1), :])
            attend(kb + 1, cur_ref, 1, bias_ref[pl.ds(kb + 1, 1), :])

        @pl.when(j % 2 == 0)
        def _():
            step(sa_ref, sb_ref)

        @pl.when(j % 2 == 1)
        def _():
            step(sb_ref, sa_ref)

        return carry

    lax.fori_loop(0, (qb + 1) // 2, body, 0)

    inv = 1.0 / l_ref[...]
    heads = []
    for h in range(nh):
        tiles = range(h * bs // MOBA_QT, (h + 1) * bs // MOBA_QT)
        heads.append(jnp.concatenate(
            [acc_ref[:, t * MOBA_QT:(t + 1) * MOBA_QT] * inv[t:t + 1, :] for t in tiles], axis=1))
    o_ref[0] = jnp.concatenate(heads, axis=0).T.astype(BF16)


def _moba(qkv):
    b, s, _ = qkv.shape
    w = ATTN_WIDTH
    bs = MOBA_BLOCK
    n_blk = s // bs
    cols = N_ATTN_HEADS * bs
    return pl.pallas_call(
        functools.partial(_moba_kernel, n_blk=n_blk),
        grid=(b, n_blk),
        in_specs=[pl.BlockSpec((1, bs, w), lambda bi, qi: (bi, qi, 0)),
                  pl.BlockSpec((1, s, w), lambda bi, qi: (bi, 0, 1)),
                  pl.BlockSpec((1, s, w), lambda bi, qi: (bi, 0, 2))],
        out_specs=pl.BlockSpec((1, bs, w), lambda bi, qi: (bi, qi, 0)),
        out_shape=jax.ShapeDtypeStruct((b, s, w), BF16),
        scratch_shapes=[pltpu.VMEM((n_blk, w), F32),
                        pltpu.VMEM((n_blk, w, bs), BF16),
                        pltpu.VMEM((bs, cols), F32),
                        pltpu.VMEM((bs, cols), F32),
                        pltpu.VMEM((w, cols), BF16),
                        pltpu.VMEM((n_blk, cols), F32),
                        pltpu.VMEM((1, bs, cols), F32),
                        pltpu.VMEM((2, bs, cols), F32),
                        pltpu.VMEM((2, bs, cols), F32),
                        pltpu.VMEM((HEAD_DIM, cols), F32),
                        pltpu.VMEM((cols // MOBA_QT, MOBA_QT), F32),
                        pltpu.VMEM((cols // MOBA_QT, MOBA_QT), F32)],
        compiler_params=pltpu.CompilerParams(
            dimension_semantics=("parallel", "arbitrary"), vmem_limit_bytes=VMEM_LIMIT),
        name="moba",
    )(qkv, qkv, qkv)


def _out_mlp_kernel(x_ref, ya_ref, yb_ref, yc_ref, wo_ref, n2_ref, w1_ref, w2_ref,
                    out_ref, h_ref, *, tf):
    tm = x_ref.shape[0]
    pr = tm // ROW_PARTS
    for s in range(ROW_PARTS):
        rows = slice(s * pr, (s + 1) * pr)
        y = jnp.concatenate([ya_ref[rows, :], yb_ref[rows, :], yc_ref[rows, :]], axis=1)
        x1 = x_ref[rows, :] + _dot(y, wo_ref[...])
        out_ref[rows, :] = x1
        ms = jnp.mean(x1 * x1, axis=-1, keepdims=True)
        h_ref[rows, :] = (x1 * lax.rsqrt(ms + EPS) * n2_ref[...]).astype(BF16)
    for c in range(D_FF // tf):
        cols = slice(c * tf, (c + 1) * tf)
        u = jnp.maximum(_dot(h_ref[...], w1_ref[:, cols]), 0.0)
        out_ref[...] += _dot((u * u).astype(BF16), w2_ref[cols, :])


def _out_mlp(x2d, ya, yb, yc, w_out, n2, w1, w2, *, tm=512, tf=1024):
    t = x2d.shape[0]
    row = lambda width: pl.BlockSpec((tm, width), lambda i: (i, 0))
    resident = lambda shape: pl.BlockSpec(shape, lambda i: (0, 0), pipeline_mode=pl.Buffered(1))
    return pl.pallas_call(
        functools.partial(_out_mlp_kernel, tf=tf),
        grid=(t // tm,),
        in_specs=[row(D_MODEL), row(ATTN_WIDTH), row(D_SSM), row(ATTN_WIDTH),
                  resident((D_MODEL, D_MODEL)), resident((1, D_MODEL)),
                  resident((D_MODEL, D_FF)), resident((D_FF, D_MODEL))],
        out_specs=row(D_MODEL),
        out_shape=jax.ShapeDtypeStruct((t, D_MODEL), F32),
        scratch_shapes=[pltpu.VMEM((tm, D_MODEL), BF16)],
        compiler_params=pltpu.CompilerParams(
            dimension_semantics=("parallel",), vmem_limit_bytes=VMEM_LIMIT),
        name="out_mlp",
    )(x2d, ya, yb, yc, w_out, n2, w1, w2)


def _layer(x2d, bsz, seq, norm1_w, w_in, a_q_norm, a_k_norm, c_q_norm, c_k_norm, conv_w, conv_b,
           dt_bias, a_log, d_skip, ssm_norm_w, w_out, norm2_w, w_mlp_in, w_mlp_out):
    w_pad = jnp.pad(w_in, ((0, 0), (0, DT_PAD - SSM_HEADS))).astype(BF16)
    scale = HEAD_DIM ** -0.5 * LOG2E
    gains = jnp.stack([jnp.tile(a_q_norm, N_ATTN_HEADS) * scale, jnp.tile(a_k_norm, N_ATTN_HEADS),
                       jnp.tile(c_q_norm, N_ATTN_HEADS) * scale, jnp.tile(c_k_norm, N_ATTN_HEADS)])
    lane = jnp.arange(ATTN_WIDTH) // HEAD_DIM
    bd = (lane[:, None] == lane[None, :]).astype(BF16)
    pad_h = lambda v: jnp.pad(v, (0, DT_PAD - SSM_HEADS)).reshape(1, DT_PAD)

    qkva, qkvc, z, xbc, dt = _in_proj(x2d, norm1_w.reshape(1, -1), w_pad, gains, bd)

    ya = _dilated(qkva.reshape(bsz, seq, -1))
    yb =_ssd(xbc.reshape(bsz, seq, -1), z.reshape(bsz, seq, -1), dt.reshape(bsz, seq, -1),
              conv_w, conv_b.reshape(1, -1), pad_h(dt_bias), pad_h(a_log),
              jnp.repeat(d_skip, HEAD_DIM).reshape(1, -1), ssm_norm_w.reshape(1, -1))
    yc = _moba(qkvc.reshape(bsz, seq, -1))

    return _out_mlp(x2d, ya.reshape(bsz * seq, -1), yb.reshape(bsz * seq, -1), yc.reshape(bsz * seq, -1),
                    w_out.astype(BF16), norm2_w.reshape(1, -1),
                    w_mlp_in.astype(BF16), w_mlp_out.astype(BF16))


def kernel(x, norm1_w, w_in, a_q_norm, a_k_norm, c_q_norm, c_k_norm, conv_w, conv_b, dt_bias,
           a_log, d_skip, ssm_norm_w, w_out, norm2_w, w_mlp_in, w_mlp_out):
    bsz, seq, d = x.shape
    x2d = x.reshape(bsz * seq, d)
    for i in range(norm1_w.shape[0]):
        x2d = _layer(x2d, bsz, seq, norm1_w[i], w_in[i], a_q_norm[i], a_k_norm[i], c_q_norm[i],
                     c_k_norm[i], conv_w[i], conv_b[i], dt_bias[i], a_log[i], d_skip[i],
                     ssm_norm_w[i], w_out[i], norm2_w[i], w_mlp_in[i], w_mlp_out[i])
    return x2d.reshape(bsz, seq, d)
```

```python
import functools

import jax
import jax.numpy as jnp
from jax import lax
from jax.experimental import pallas as pl
from jax.experimental.pallas import tpu as pltpu

F32 = jnp.float32
BF16 = jnp.bfloat16

D_MODEL = 1024
HEAD_DIM = 64
N_ATTN_HEADS = 4
ATTN_WIDTH = N_ATTN_HEADS * HEAD_DIM
DILATED_PATTERNS = ((128, 1), (512, 4), (2048, 16))
BAND = 128
MOBA_BLOCK = 256
MOBA_TOPK = 3
SSM_HEADS = 8
D_SSM = SSM_HEADS * HEAD_DIM
SSM_STATE = 128
SSM_GROUPS = 2
SSM_CONV = 4
SSM_CHUNK = 128
CONV_DIM = D_SSM + 2 * SSM_GROUPS * SSM_STATE
D_FF = 4 * D_MODEL
D_IN = 2 * 3 * ATTN_WIDTH + D_SSM + CONV_DIM + SSM_HEADS
LANES = 128
DT_PAD = LANES
D_IN_PAD = D_IN - SSM_HEADS + DT_PAD
EPS = 1e-6
NEG_INF = -1e30
LOG2E = 1.4426950408889634
SLOPES_A = tuple(2.0 ** -(2 * h + 1) for h in range(N_ATTN_HEADS))
SLOPES_C = tuple(2.0 ** -(2 * h + 2) for h in range(N_ATTN_HEADS))

VMEM_LIMIT = 56 * 1024 * 1024
ROW_PARTS = 4


def _split3(a):
    hi = a.astype(BF16)
    r1 = a - hi.astype(F32)
    mid = r1.astype(BF16)
    lo = (r1 - mid.astype(F32)).astype(BF16)
    return hi, mid, lo


def _dot(a, b):
    return jnp.dot(a, b, preferred_element_type=F32)


def _dot_nt(a, b):
    return lax.dot_general(a, b, (((1,), (1,)), ((), ())), preferred_element_type=F32)


def _dot_tn(a, b):
    return lax.dot_general(a, b, (((0,), (0,)), ((), ())), preferred_element_type=F32)


def _dot_f32_lhs(a, sel):
    hi, mid, lo = _split3(a)
    return _dot(hi, sel) + _dot(mid, sel) + _dot(lo, sel)


def _dot_f32_rhs(sel, b):
    hi, mid, lo = _split3(b)
    return _dot(sel, hi) + _dot(sel, mid) + _dot(sel, lo)


VT_ROWS = HEAD_DIM + 16


def _vt_with_ones(vt):
    keys = vt.shape[1]
    ones_tile = (lax.broadcasted_iota(jnp.int32, (VT_ROWS - HEAD_DIM, keys), 0) == 0).astype(F32)
    parts = []
    for h in range(vt.shape[0] // HEAD_DIM):
        parts += [vt[h * HEAD_DIM:(h + 1) * HEAD_DIM, :], ones_tile]
    return jnp.concatenate(parts, axis=0).astype(BF16)


def _head_masks(width, dtype):
    lane = lax.broadcasted_iota(jnp.int32, (1, width), 1)
    return [(lane // HEAD_DIM == h).astype(dtype) for h in range(width // HEAD_DIM)]


def _row_slopes(rows_per_head, slopes):
    row = lax.broadcasted_iota(jnp.int32, (len(slopes) * rows_per_head, 1), 0) // rows_per_head
    out = jnp.full(row.shape, slopes[-1], F32)
    for h in range(len(slopes) - 2, -1, -1):
        out = jnp.where(row == h, slopes[h], out)
    return out


def _in_proj_kernel(x_ref, n1_ref, w_ref, g_ref, bd_ref,
                    qkva_ref, qkvc_ref, z_ref, xbc_ref, dt_ref, h_ref):
    pr = x_ref.shape[0] // ROW_PARTS
    for s in range(ROW_PARTS):
        rows = slice(s * pr, (s + 1) * pr)
        x = x_ref[rows, :]
        ms = jnp.mean(x * x, axis=-1, keepdims=True)
        h_ref[rows, :] = (x * lax.rsqrt(ms + EPS) * n1_ref[...]).astype(BF16)
    bd = bd_ref[...]

    def seg(lo, width):
        return _dot(h_ref[...], w_ref[:, lo:lo + width])

    def head_norm(y, g):
        ss = _dot((y * y).astype(BF16), bd)
        return y * lax.rsqrt(ss * (1.0 / HEAD_DIM) + EPS) * g

    w = ATTN_WIDTH
    for i, out_ref in enumerate((qkva_ref, qkvc_ref)):
        base = 3 * w * i
        out_ref[:, 0:w] = head_norm(seg(base, w), g_ref[2 * i:2 * i + 1, :]).astype(BF16)
        out_ref[:, w:2 * w] = head_norm(seg(base + w, w), g_ref[2 * i + 1:2 * i + 2, :]).astype(BF16)
        out_ref[:, 2 * w:3 * w] = seg(base + 2 * w, w).astype(BF16)
    z_ref[...] = seg(6 * w, D_SSM)
    xbc_ref[...] = seg(6 * w + D_SSM, CONV_DIM)
    dt_ref[...] = seg(6 * w + D_SSM + CONV_DIM, DT_PAD)


def _in_proj(x2d, n1, w_pad, gains, bd, *, tm=512):
    t = x2d.shape[0]
    row = lambda i: (i, 0)
    resident = lambda shape: pl.BlockSpec(shape, lambda i: (0, 0), pipeline_mode=pl.Buffered(1))
    return pl.pallas_call(
        _in_proj_kernel,
        grid=(t // tm,),
        in_specs=[
            pl.BlockSpec((tm, D_MODEL), row),
            resident((1, D_MODEL)),
            resident((D_MODEL, D_IN_PAD)),
            resident((4, ATTN_WIDTH)),
            resident((ATTN_WIDTH, ATTN_WIDTH)),
        ],
        out_specs=[
            pl.BlockSpec((tm, 3 * ATTN_WIDTH), row),
            pl.BlockSpec((tm, 3 * ATTN_WIDTH), row),
            pl.BlockSpec((tm, D_SSM), row),
            pl.BlockSpec((tm, CONV_DIM), row),
            pl.BlockSpec((tm, DT_PAD), row),
        ],
        out_shape=[
            jax.ShapeDtypeStruct((t, 3 * ATTN_WIDTH), BF16),
            jax.ShapeDtypeStruct((t, 3 * ATTN_WIDTH), BF16),
            jax.ShapeDtypeStruct((t, D_SSM), F32),
            jax.ShapeDtypeStruct((t, CONV_DIM), F32),
            jax.ShapeDtypeStruct((t, DT_PAD), F32),
        ],
        scratch_shapes=[pltpu.VMEM((tm, D_MODEL), BF16)],
        compiler_params=pltpu.CompilerParams(
            dimension_semantics=("parallel",), vmem_limit_bytes=VMEM_LIMIT),
        name="in_proj",
    )(x2d, n1, w_pad, gains, bd)


N_SLABS = 3 * ATTN_WIDTH // LANES
COPY_ROWS = 512
GATHER_UNROLL = 4


def _dilated_kernel(qkv_ref, o_ref, slab_ref, qr_ref, kr_ref, vt_ref, bias_ref, sa_ref, sb_ref,
                    st_ref, y_ref, lse_ref):
    nh = N_ATTN_HEADS
    w = ATTN_WIDTH
    seq = qkv_ref.shape[1]
    n_tiles = seq // BAND
    hm_b = _head_masks(w, BF16)

    @pl.when(pl.program_id(0) == 0)
    def _():
        c = lax.broadcasted_iota(jnp.int32, (2 * BAND, nh * BAND), 0)
        col = lax.broadcasted_iota(jnp.int32, (2 * BAND, nh * BAND), 1)
        a = col % BAND
        head = col // BAND
        slope = jnp.full(col.shape, SLOPES_A[-1], F32)
        for h in range(nh - 2, -1, -1):
            slope = jnp.where(head == h, SLOPES_A[h], slope)
        for p, (_, dil) in enumerate(DILATED_PATTERNS):
            for later, offset in enumerate((a - c, a + BAND - c)):
                valid = (offset >= 0) & (offset <= BAND)
                bias_ref[p, later] = jnp.where(valid, -(slope * (LOG2E * dil)) * offset.astype(F32), NEG_INF)

    def copy(i, carry):
        rows = pl.ds(pl.multiple_of(i * COPY_ROWS, COPY_ROWS), COPY_ROWS)
        x = qkv_ref[0, rows, :].astype(F32)
        for j in range(N_SLABS):
            slab_ref[j, rows, :] = x[:, j * LANES:(j + 1) * LANES]
        return carry
    lax.fori_loop(0, seq // COPY_ROWS, copy, 0)

    def run(p, dil, first_pattern):
        nblk = seq // dil // BAND

        def sub_rows(t):
            if dil == 1:
                return pl.ds(pl.multiple_of(t * BAND, BAND), BAND)
            return pl.ds(t // nblk + dil * BAND * (t % nblk), BAND, stride=dil)

        def gather(i, carry):
            for u in range(GATHER_UNROLL):
                t = i * GATHER_UNROLL + u
                parts = [slab_ref[j, sub_rows(t), :] for j in range(N_SLABS)]
                dst = pl.ds(pl.multiple_of(t * BAND, BAND), BAND)
                qr_ref[dst, :] = jnp.concatenate(parts[0:2], axis=1).astype(BF16)
                kr_ref[dst, :] = jnp.concatenate(parts[2:4], axis=1).astype(BF16)
                vt_ref[t] = _vt_with_ones(jnp.concatenate(parts[4:6], axis=1).T)
            return carry
        lax.fori_loop(0, n_tiles // GATHER_UNROLL, gather, 0)

        def scores(t, dst_ref):
            tt = jnp.minimum(t, n_tiles - 1)
            later = jnp.minimum(tt % nblk, 1)
            q = qr_ref[pl.ds(pl.multiple_of(tt * BAND, BAND), BAND), :]
            q4 = jnp.concatenate([q * hm_b[h] for h in range(nh)], axis=0)
            k0 = pl.multiple_of((tt - later) * BAND, BAND)
            dst_ref[...] = _dot_nt(kr_ref[pl.ds(k0, 2 * BAND), :], q4) + bias_ref[p, later]

        def softmax_pv(t, src_ref, slot):
            kb = t - jnp.minimum(t % nblk, 1)
            outs, lses = [], []
            for h in range(nh):
                cs = slice(h * BAND, (h + 1) * BAND)
                hr = slice(h * VT_ROWS, (h + 1) * VT_ROWS)
                mx = jnp.max(src_ref[:, cs], axis=0, keepdims=True)
                pexp = jnp.exp2(src_ref[:, cs] - mx)
                vt = jnp.concatenate([vt_ref[kb, hr, :], vt_ref[kb + 1, hr, :]], axis=1)
                pv = _dot(vt, pexp.astype(BF16))
                l = pv[HEAD_DIM:HEAD_DIM + 1, :]
                outs.append(pv[0:HEAD_DIM, :] * (1.0 / l))
                lses.append(jnp.broadcast_to(mx + jnp.log2(l), (HEAD_DIM, BAND)))
            st_ref[slot, 0] = jnp.concatenate(outs, axis=0)
            st_ref[slot, 1] = jnp.concatenate(lses, axis=0)

        def merge(m, slot):
            out = st_ref[slot, 0].T
            lse = st_ref[slot, 1].T
            rows = sub_rows(jnp.maximum(m, 0))
            for sl in range(w // LANES):
                ls = slice(sl * LANES, (sl + 1) * LANES)
                if first_pattern:
                    y_ref[sl, rows, :] = out[:, ls]
                    lse_ref[sl, rows, :] = lse[:, ls]
                else:
                    l_old = lse_ref[sl, rows, :]
                    mx2 = jnp.maximum(l_old, lse[:, ls])
                    e_old = jnp.exp2(l_old - mx2)
                    e_new = jnp.exp2(lse[:, ls] - mx2)
                    den = e_old + e_new
                    y_ref[sl, rows, :] = (y_ref[sl, rows, :] * e_old + out[:, ls] * e_new) / den
                    lse_ref[sl, rows, :] = mx2 + jnp.log2(den)

        scores(0, sa_ref)
        st_ref[1, 0] = jnp.zeros(st_ref.shape[2:], F32)
        st_ref[1, 1] = jnp.full(st_ref.shape[2:], NEG_INF, F32)

        def pair(i, carry):
            t = 2 * i
            scores(t + 1, sb_ref)
            softmax_pv(t, sa_ref, 0)
            merge(t - 1, 1)
            scores(t + 2, sa_ref)
            softmax_pv(t + 1, sb_ref, 1)
            merge(t, 0)
            return carry
        lax.fori_loop(0, n_tiles // 2, pair, 0)
        merge(n_tiles - 1, 1)

    order = sorted(range(len(DILATED_PATTERNS)), key=lambda p: -DILATED_PATTERNS[p][1])
    for p in order:
        run(p, DILATED_PATTERNS[p][1], p == order[0])

    def emit(i, carry):
        rows = pl.ds(pl.multiple_of(i * COPY_ROWS, COPY_ROWS), COPY_ROWS)
        o_ref[0, rows, :] = jnp.concatenate(
            [y_ref[sl, rows, :] for sl in range(w // LANES)], axis=1).astype(BF16)
        return carry
    lax.fori_loop(0, seq // COPY_ROWS, emit, 0)


def _dilated(qkv):
    b, s, _ = qkv.shape
    w = ATTN_WIDTH
    cols = N_ATTN_HEADS * BAND
    return pl.pallas_call(
        _dilated_kernel,
        grid=(b,),
        in_specs=[pl.BlockSpec((1, s, 3 * w), lambda bi: (bi, 0, 0))],
        out_specs=pl.BlockSpec((1, s, w), lambda bi: (bi, 0, 0)),
        out_shape=jax.ShapeDtypeStruct((b, s, w), BF16),
        scratch_shapes=[pltpu.VMEM((N_SLABS, s, LANES), F32),
                        pltpu.VMEM((s, w), BF16),
                        pltpu.VMEM((s, w), BF16),
                        pltpu.VMEM((s // BAND, N_ATTN_HEADS * VT_ROWS, BAND), BF16),
                        pltpu.VMEM((len(DILATED_PATTERNS), 2, 2 * BAND, cols), F32),
                        pltpu.VMEM((2 * BAND, cols), F32),
                        pltpu.VMEM((2 * BAND, cols), F32),
                        pltpu.VMEM((2, 2, w, BAND), F32),
                        pltpu.VMEM((w // LANES, s, LANES), F32),
                        pltpu.VMEM((w // LANES, s, LANES), F32)],
        compiler_params=pltpu.CompilerParams(
            dimension_semantics=("arbitrary",), vmem_limit_bytes=VMEM_LIMIT),
        name="dilated",
    )(qkv)


def _softplus(x):
    return jnp.maximum(x, 0.0) + jnp.log1p(jnp.exp(-jnp.abs(x)))


def _silu(x):
    return x / (1.0 + jnp.exp(-x))


def _ssd_kernel(xbc_ref, z_ref, dt_ref, cw_ref, cb_ref, dtb_ref, alog_ref, dsk_ref, nw_ref,
                y_ref, ext_ref, tail_ref, state_ref, *, chunks):
    rows = chunks * SSM_CHUNK
    halo = 8
    cl = SSM_CHUNK
    gw = D_SSM // SSM_GROUPS
    hpg = SSM_HEADS // SSM_GROUPS

    @pl.when(pl.program_id(1) == 0)
    def _():
        tail_ref[...] = jnp.zeros_like(tail_ref)
        state_ref[...] = jnp.zeros_like(state_ref)

    ext_ref[0:halo, :] = tail_ref[...]
    ext_ref[halo:halo + rows, :] = xbc_ref[0]
    tail_ref[...] = xbc_ref[0, rows - halo:rows, :]
    conv = cb_ref[...] + cw_ref[SSM_CONV - 1:SSM_CONV, :] * ext_ref[halo:halo + rows, :]
    for j in range(1, SSM_CONV):
        conv = conv + cw_ref[SSM_CONV - 1 - j:SSM_CONV - j, :] * ext_ref[halo - j:halo - j + rows, :]
    xact = _silu(conv)

    dt = _softplus(dt_ref[0] + dtb_ref[...])
    da = dt * (-jnp.exp(alog_ref[...]))

    ri = lax.broadcasted_iota(jnp.int32, (cl, cl), 0)
    ci = lax.broadcasted_iota(jnp.int32, (cl, cl), 1)
    tri = ri >= ci
    tril = tri.astype(BF16)
    eh = lax.broadcasted_iota(jnp.int32, (DT_PAD, D_SSM), 0)
    el = lax.broadcasted_iota(jnp.int32, (DT_PAD, D_SSM), 1) // HEAD_DIM
    expand = (eh == el).astype(BF16)
    hm_g = _head_masks(gw, F32)

    for c in range(chunks):
        r0 = c * cl
        xs = xact[r0:r0 + cl, 0:D_SSM]
        bm = xact[r0:r0 + cl, D_SSM:D_SSM + SSM_GROUPS * SSM_STATE]
        cm = xact[r0:r0 + cl, D_SSM + SSM_GROUPS * SSM_STATE:]
        acs = _dot_f32_rhs(tril, da[r0:r0 + cl])
        acs_t = acs.T
        acs_full = _dot_f32_lhs(acs, expand)
        dt_full = _dot_f32_lhs(dt[r0:r0 + cl], expand)
        xd = xs * dt_full
        last = acs_full[cl - 1:cl, :]
        xdd = xd * jnp.exp(last - acs_full)
        exp_acs = jnp.exp(acs_full)
        chunk_decay = jnp.exp(last)
        y_parts = []
        for g in range(SSM_GROUPS):
            gs = slice(g * gw, (g + 1) * gw)
            bg = bm[:, g * SSM_STATE:(g + 1) * SSM_STATE].astype(BF16)
            cg = cm[:, g * SSM_STATE:(g + 1) * SSM_STATE].astype(BF16)
            cb = _dot_nt(cg, bg)
            s_in = state_ref[:, gs]
            y_g = _dot(cg, s_in.astype(BF16)) * exp_acs[:, gs]
            state_ref[:, gs] = s_in * chunk_decay[:, gs] + _dot_tn(bg, xdd[:, gs].astype(BF16))
            xd_g = xd[:, gs]
            for hh in range(hpg):
                h = g * hpg + hh
                diff = acs[:, h:h + 1] - acs_t[h:h + 1, :]
                lmat = jnp.where(tri, jnp.exp(jnp.where(tri, diff, 0.0)), 0.0)
                y_g = y_g + _dot((cb * lmat).astype(BF16), (xd_g * hm_g[hh]).astype(BF16))
            y_parts.append(y_g)
        y = jnp.concatenate(y_parts, axis=1) + dsk_ref[...] * xs
        y = y * _silu(z_ref[0, r0:r0 + cl, :])
        outs = []
        for g in range(SSM_GROUPS):
            yg = y[:, g * gw:(g + 1) * gw]
            ms = jnp.mean(yg * yg, axis=-1, keepdims=True)
            outs.append(yg * lax.rsqrt(ms + EPS) * nw_ref[:, g * gw:(g + 1) * gw])
        y_ref[0, r0:r0 + cl, :] = jnp.concatenate(outs, axis=1).astype(BF16)


def _ssd(xbc, z, dt, conv_w, conv_b, dt_bias, a_log, d_skip_full, norm_w, *, chunks=4):
    b, s, _ = xbc.shape
    rows = chunks * SSM_CHUNK
    blk = lambda width: pl.BlockSpec((1, rows, width), lambda bi, ci: (bi, ci, 0))
    const = lambda shape: pl.BlockSpec(shape, lambda bi, ci: (0, 0))
    return pl.pallas_call(
        functools.partial(_ssd_kernel, chunks=chunks),
        grid=(b, s // rows),
        in_specs=[blk(CONV_DIM), blk(D_SSM), blk(DT_PAD),
                  const((SSM_CONV, CONV_DIM)), const((1, CONV_DIM)), const((1, DT_PAD)),
                  const((1, DT_PAD)), const((1, D_SSM)), const((1, D_SSM))],
        out_specs=blk(D_SSM),
        out_shape=jax.ShapeDtypeStruct((b, s, D_SSM), BF16),
        scratch_shapes=[pltpu.VMEM((rows + 8, CONV_DIM), F32),
                        pltpu.VMEM((8, CONV_DIM), F32),
                        pltpu.VMEM((SSM_STATE, D_SSM), F32)],
        compiler_params=pltpu.CompilerParams(
            dimension_semantics=("parallel", "arbitrary"), vmem_limit_bytes=VMEM_LIMIT),
        name="ssd",
    )(xbc, z, dt, conv_w, conv_b, dt_bias, a_log, d_skip_full, norm_w)


MOBA_QT = LANES


def _moba_kernel(q_ref, k_ref, v_ref, o_ref, kmean_ref, vt_ref, rel_ref, relown_ref, q4t_ref,
                 bias_ref, so_ref, sa_ref, sb_ref, acc_ref, m_ref, l_ref, *, n_blk):
    nh = N_ATTN_HEADS
    bs = MOBA_BLOCK
    cols = nh * bs
    qb = pl.program_id(1)

    col = lax.broadcasted_iota(jnp.int32, (1, cols), 1)
    col_head = col // bs
    slope = jnp.full((1, cols), SLOPES_C[-1], F32)
    for h in range(nh - 2, -1, -1):
        slope = jnp.where(col_head == h, SLOPES_C[h], slope)

    @pl.when(qb == 0)
    def _():
        for n in range(n_blk):
            kblk = k_ref[0, n * bs:(n + 1) * bs, :].astype(F32)
            kmean_ref[n:n + 1, :] = jnp.sum(kblk, axis=0, keepdims=True) * (1.0 / bs)
            vt_ref[n] = _vt_with_ones(v_ref[0, n * bs:(n + 1) * bs, :].astype(F32).T)
        c = lax.broadcasted_iota(jnp.int32, (bs, cols), 0)
        a = lax.broadcasted_iota(jnp.int32, (bs, cols), 1) % bs
        rel = -(slope * LOG2E) * (a - c).astype(F32)
        rel_ref[...] = rel
        relown_ref[...] = jnp.where(a >= c, rel, NEG_INF)

    qt = q_ref[0].astype(F32).T
    frow = lax.broadcasted_iota(jnp.int32, (ATTN_WIDTH, bs), 0) // HEAD_DIM
    for h in range(nh):
        q4t_ref[:, h * bs:(h + 1) * bs] = jnp.where(frow == h, qt, 0.0).astype(BF16)

    def block_gate():
        km = kmean_ref[...]
        km_hi = km.astype(BF16)
        km_lo = (km - km_hi.astype(F32)).astype(BF16)
        q4t = q4t_ref[...]
        return _dot(km_hi, q4t) + _dot(km_lo, q4t)

    def select_blocks(gate):
        blk =lax.broadcasted_iota(jnp.int32, (n_blk, cols), 0)
        blk_f = blk.astype(F32)
        past = blk < qb
        g = jnp.where(past, gate, NEG_INF)
        sel = jnp.zeros((n_blk, cols), jnp.bool_)
        for _ in range(MOBA_TOPK):
            mx = jnp.max(g, axis=0, keepdims=True)
            first = jnp.min(jnp.where(g == mx, blk_f, float(n_blk)), axis=0, keepdims=True)
            pick = blk_f == first
            sel = sel | pick
            g = jnp.where(pick, -jnp.inf, g)
        sel = sel & past
        bias_ref[...] = jnp.where(sel, -(slope * (LOG2E * bs)) * (qb - blk).astype(F32), NEG_INF)

    def scores(kb, dst_ref, slot, rel_mat_ref):
        kb = jnp.minimum(kb, n_blk - 1)
        kblk = k_ref[0, pl.ds(pl.multiple_of(kb * bs, bs), bs), :]
        for h in range(nh):
            hs = slice(h * bs, (h + 1) * bs)
            dst_ref[slot, :, hs] = _dot(kblk, q4t_ref[:, hs]) + rel_mat_ref[:, hs]

    def attend(kb, src_ref, slot, bias_row):
        first = bias_row is None
        for h in range(nh):
            hs = slice(h * bs, (h + 1) * bs)
            ps, alphas = [], []
            tiles = range(h * bs // MOBA_QT, (h + 1) * bs // MOBA_QT)
            for t in tiles:
                cs = slice(t * MOBA_QT, (t + 1) * MOBA_QT)
                mx = jnp.max(src_ref[slot, :, cs], axis=0, keepdims=True)
                s = src_ref[slot, :, cs]
                if first:
                    m_new = shift = mx
                else:
                    b = bias_row[:, cs]
                    m_old = m_ref[t:t + 1, :]
                    m_new = jnp.maximum(m_old, mx + b)
                    shift = m_new - b
                    alphas.append(jnp.exp2(m_old - m_new))
                m_ref[t:t + 1, :] = m_new
                ps.append(jnp.exp2(s - shift).astype(BF16))
            pv = _dot(vt_ref[kb, h * VT_ROWS:(h + 1) * VT_ROWS, :], jnp.concatenate(ps, axis=1))
            for i, t in enumerate(tiles):
                lsum = pv[HEAD_DIM:HEAD_DIM + 1, i * MOBA_QT:(i + 1) * MOBA_QT]
                l_ref[t:t + 1, :] = lsum if first else alphas[i] * l_ref[t:t + 1, :] + lsum
            pv = pv[0:HEAD_DIM, :]
            acc_ref[:, hs] = pv if first else jnp.concatenate(alphas, axis=1) * acc_ref[:, hs] + pv

    gate = block_gate()
    scores(qb, so_ref, 0, relown_ref)
    scores(0, sa_ref, 0, rel_ref)
    scores(1, sa_ref, 1, rel_ref)
    attend(qb, so_ref, 0, None)
    select_blocks(gate)

    def body(j, carry):
        kb = 2 * j

        def step(cur_ref, nxt_ref):
            scores(kb + 2, nxt_ref, 0, rel_ref)
            scores(kb + 3, nxt_ref, 1, rel_ref)
            attend(kb, cur_ref, 0, bias_ref[pl.ds(kb, 1), :])
            attend(kb + 1, cur_ref, 1, bias_ref[pl.ds(kb + 1, 1), :])

        @pl.when(j % 2 == 0)
        def _():
            step(sa_ref, sb_ref)

        @pl.when(j % 2 == 1)
        def _():
            step(sb_ref, sa_ref)

        return carry

    n_pairs = qb // 2
    lax.fori_loop(0, n_pairs, body, 0)

    for parity, cur_ref in ((0, sa_ref), (1, sb_ref)):
        @pl.when((qb % 2 == 1) & (n_pairs % 2 == parity))
        def _(cur_ref=cur_ref):
            attend(qb - 1, cur_ref, 0, bias_ref[pl.ds(qb - 1, 1), :])

    inv = 1.0 / l_ref[...]
    heads = []
    for h in range(nh):
        tiles = range(h * bs // MOBA_QT, (h + 1) * bs // MOBA_QT)
        heads.append(jnp.concatenate(
            [acc_ref[:, t * MOBA_QT:(t + 1) * MOBA_QT] * inv[t:t + 1, :] for t in tiles], axis=1))
    o_ref[0] = jnp.concatenate(heads, axis=0).T.astype(BF16)


def _moba(qkv):
    b, s, _ = qkv.shape
    w = ATTN_WIDTH
    bs = MOBA_BLOCK
    n_blk = s // bs
    cols = N_ATTN_HEADS * bs
    return pl.pallas_call(
        functools.partial(_moba_kernel, n_blk=n_blk),
        grid=(b, n_blk),
        in_specs=[pl.BlockSpec((1, bs, w), lambda bi, qi: (bi, qi, 0)),
                  pl.BlockSpec((1, s, w), lambda bi, qi: (bi, 0, 1)),
                  pl.BlockSpec((1, s, w), lambda bi, qi: (bi, 0, 2))],
        out_specs=pl.BlockSpec((1, bs, w), lambda bi, qi: (bi, qi, 0)),
        out_shape=jax.ShapeDtypeStruct((b, s, w), BF16),
        scratch_shapes=[pltpu.VMEM((n_blk, w), F32),
                        pltpu.VMEM((n_blk, N_ATTN_HEADS * VT_ROWS, bs), BF16),
                        pltpu.VMEM((bs, cols), F32),
                        pltpu.VMEM((bs, cols), F32),
                        pltpu.VMEM((w, cols), BF16),
                        pltpu.VMEM((n_blk, cols), F32),
                        pltpu.VMEM((1, bs, cols), F32),
                        pltpu.VMEM((2, bs, cols), F32),
                        pltpu.VMEM((2, bs, cols), F32),
                        pltpu.VMEM((HEAD_DIM, cols), F32),
                        pltpu.VMEM((cols // MOBA_QT, MOBA_QT), F32),
                        pltpu.VMEM((cols // MOBA_QT, MOBA_QT), F32)],
        compiler_params=pltpu.CompilerParams(
            dimension_semantics=("parallel", "arbitrary"), vmem_limit_bytes=VMEM_LIMIT),
        name="moba",
    )(qkv, qkv, qkv)


def _out_mlp_kernel(x_ref, ya_ref, yb_ref, yc_ref, wo_ref, n2_ref, w1_ref, w2_ref,
                    out_ref, h_ref, *, tf):
    tm = x_ref.shape[0]
    pr = tm // ROW_PARTS
    for s in range(ROW_PARTS):
        rows = slice(s * pr, (s + 1) * pr)
        y = jnp.concatenate([ya_ref[rows, :], yb_ref[rows, :], yc_ref[rows, :]], axis=1)
        x1 = x_ref[rows, :] + _dot(y, wo_ref[...])
        out_ref[rows, :] = x1
        ms = jnp.mean(x1 * x1, axis=-1, keepdims=True)
        h_ref[rows, :] = (x1 * lax.rsqrt(ms + EPS) * n2_ref[...]).astype(BF16)
    for c in range(D_FF // tf):
        cols = slice(c * tf, (c + 1) * tf)
        u = jnp.maximum(_dot(h_ref[...], w1_ref[:, cols]), 0.0)
        out_ref[...] += _dot((u * u).astype(BF16), w2_ref[cols, :])


def _out_mlp(x2d, ya, yb, yc, w_out, n2, w1, w2, *, tm=512, tf=1024):
    t = x2d.shape[0]
    row = lambda width: pl.BlockSpec((tm, width), lambda i: (i, 0))
    resident = lambda shape: pl.BlockSpec(shape, lambda i: (0, 0), pipeline_mode=pl.Buffered(1))
    return pl.pallas_call(
        functools.partial(_out_mlp_kernel, tf=tf),
        grid=(t // tm,),
        in_specs=[row(D_MODEL), row(ATTN_WIDTH), row(D_SSM), row(ATTN_WIDTH),
                  resident((D_MODEL, D_MODEL)), resident((1, D_MODEL)),
                  resident((D_MODEL, D_FF)), resident((D_FF, D_MODEL))],
        out_specs=row(D_MODEL),
        out_shape=jax.ShapeDtypeStruct((t, D_MODEL), F32),
        scratch_shapes=[pltpu.VMEM((tm, D_MODEL), BF16)],
        compiler_params=pltpu.CompilerParams(
            dimension_semantics=("parallel",), vmem_limit_bytes=VMEM_LIMIT),
        name="out_mlp",
    )(x2d, ya, yb, yc, w_out, n2, w1, w2)


def _layer(x2d, bsz, seq, norm1_w, w_in, a_q_norm, a_k_norm, c_q_norm, c_k_norm, conv_w, conv_b,
           dt_bias, a_log, d_skip, ssm_norm_w, w_out, norm2_w, w_mlp_in, w_mlp_out):
    w_pad = jnp.pad(w_in, ((0, 0), (0, DT_PAD - SSM_HEADS))).astype(BF16)
    scale = HEAD_DIM ** -0.5 * LOG2E
    gains = jnp.stack([jnp.tile(a_q_norm, N_ATTN_HEADS) * scale, jnp.tile(a_k_norm, N_ATTN_HEADS),
                       jnp.tile(c_q_norm, N_ATTN_HEADS) * scale, jnp.tile(c_k_norm, N_ATTN_HEADS)])
    lane = jnp.arange(ATTN_WIDTH) // HEAD_DIM
    bd = (lane[:, None] == lane[None, :]).astype(BF16)
    pad_h = lambda v: jnp.pad(v, (0, DT_PAD - SSM_HEADS)).reshape(1, DT_PAD)

    qkva, qkvc, z, xbc, dt = _in_proj(x2d, norm1_w.reshape(1, -1), w_pad, gains, bd)

    ya = _dilated(qkva.reshape(bsz, seq, -1))
    yb =_ssd(xbc.reshape(bsz, seq, -1), z.reshape(bsz, seq, -1), dt.reshape(bsz, seq, -1),
              conv_w, conv_b.reshape(1, -1), pad_h(dt_bias), pad_h(a_log),
              jnp.repeat(d_skip, HEAD_DIM).reshape(1, -1), ssm_norm_w.reshape(1, -1))
    yc = _moba(qkvc.reshape(bsz, seq, -1))

    return _out_mlp(x2d, ya.reshape(bsz * seq, -1), yb.reshape(bsz * seq, -1), yc.reshape(bsz * seq, -1),
                    w_out.astype(BF16), norm2_w.reshape(1, -1),
                    w_mlp_in.astype(BF16), w_mlp_out.astype(BF16))


def kernel(x, norm1_w, w_in, a_q_norm, a_k_norm, c_q_norm, c_k_norm, conv_w, conv_b, dt_bias,
           a_log, d_skip, ssm_norm_w, w_out, norm2_w, w_mlp_in, w_mlp_out):
    bsz, seq, d = x.shape
    x2d = x.reshape(bsz * seq, d)
    for i in range(norm1_w.shape[0]):
        x2d = _layer(x2d, bsz, seq, norm1_w[i], w_in[i], a_q_norm[i], a_k_norm[i], c_q_norm[i],
                     c_k_norm[i], conv_w[i], conv_b[i], dt_bias[i], a_log[i], d_skip[i],
                     ssm_norm_w[i], w_out[i], norm2_w[i], w_mlp_in[i], w_mlp_out[i])
    return x2d.reshape(bsz, seq, d)
```

```python
import functools

import jax
import jax.numpy as jnp
from jax import lax
from jax.experimental import pallas as pl
from jax.experimental.pallas import tpu as pltpu

F32 = jnp.float32
BF16 = jnp.bfloat16

D_MODEL = 1024
HEAD_DIM = 64
N_ATTN_HEADS = 4
ATTN_WIDTH = N_ATTN_HEADS * HEAD_DIM
DILATED_PATTERNS = ((128, 1), (512, 4), (2048, 16))
BAND = 128
MOBA_BLOCK = 256
MOBA_TOPK = 3
SSM_HEADS = 8
D_SSM = SSM_HEADS * HEAD_DIM
SSM_STATE = 128
SSM_GROUPS = 2
SSM_CONV = 4
SSM_CHUNK = 128
CONV_DIM = D_SSM + 2 * SSM_GROUPS * SSM_STATE
D_FF = 4 * D_MODEL
D_IN = 2 * 3 * ATTN_WIDTH + D_SSM + CONV_DIM + SSM_HEADS
LANES = 128
DT_PAD = LANES
D_IN_PAD = D_IN - SSM_HEADS + DT_PAD
EPS = 1e-6
NEG_INF = -1e30
LOG2E = 1.4426950408889634
SLOPES_A = tuple(2.0 ** -(2 * h + 1) for h in range(N_ATTN_HEADS))
SLOPES_C = tuple(2.0 ** -(2 * h + 2) for h in range(N_ATTN_HEADS))

VMEM_LIMIT = 56 * 1024 * 1024
ROW_PARTS = 4


def _split3(a):
    hi = a.astype(BF16)
    r1 = a - hi.astype(F32)
    mid = r1.astype(BF16)
    lo = (r1 - mid.astype(F32)).astype(BF16)
    return hi, mid, lo


def _dot(a, b):
    return jnp.dot(a, b, preferred_element_type=F32)


def _dot_nt(a, b):
    return lax.dot_general(a, b, (((1,), (1,)), ((), ())), preferred_element_type=F32)


def _dot_tn(a, b):
    return lax.dot_general(a, b, (((0,), (0,)), ((), ())), preferred_element_type=F32)


def _dot_f32_lhs(a, sel):
    hi, mid, lo = _split3(a)
    return _dot(hi, sel) + _dot(mid, sel) + _dot(lo, sel)


def _dot_f32_rhs(sel, b):
    hi, mid, lo = _split3(b)
    return _dot(sel, hi) + _dot(sel, mid) + _dot(sel, lo)


VT_ROWS = HEAD_DIM + 16


def _vt_with_ones(vt):
    keys = vt.shape[1]
    ones_tile = (lax.broadcasted_iota(jnp.int32, (VT_ROWS - HEAD_DIM, keys), 0) == 0).astype(F32)
    parts = []
    for h in range(vt.shape[0] // HEAD_DIM):
        parts += [vt[h * HEAD_DIM:(h + 1) * HEAD_DIM, :], ones_tile]
    return jnp.concatenate(parts, axis=0).astype(BF16)


def _head_masks(width, dtype):
    lane = lax.broadcasted_iota(jnp.int32, (1, width), 1)
    return [(lane // HEAD_DIM == h).astype(dtype) for h in range(width // HEAD_DIM)]


def _row_slopes(rows_per_head, slopes):
    row = lax.broadcasted_iota(jnp.int32, (len(slopes) * rows_per_head, 1), 0) // rows_per_head
    out = jnp.full(row.shape, slopes[-1], F32)
    for h in range(len(slopes) - 2, -1, -1):
        out = jnp.where(row == h, slopes[h], out)
    return out


def _in_proj_kernel(x_ref, n1_ref, w_ref, g_ref, bd_ref,
                    qkva_ref, qkvc_ref, z_ref, xbc_ref, dt_ref, h_ref):
    pr = x_ref.shape[0] // ROW_PARTS
    for s in range(ROW_PARTS):
        rows = slice(s * pr, (s + 1) * pr)
        x = x_ref[rows, :]
        ms = jnp.mean(x * x, axis=-1, keepdims=True)
        h_ref[rows, :] = (x * lax.rsqrt(ms + EPS) * n1_ref[...]).astype(BF16)
    bd = bd_ref[...]

    def seg(lo, width):
        return _dot(h_ref[...], w_ref[:, lo:lo + width])

    def head_norm(y, g):
        ss = _dot((y * y).astype(BF16), bd)
        return y * lax.rsqrt(ss * (1.0 / HEAD_DIM) + EPS) * g

    w = ATTN_WIDTH
    for i, out_ref in enumerate((qkva_ref, qkvc_ref)):
        base = 3 * w * i
        out_ref[:, 0:w] = head_norm(seg(base, w), g_ref[2 * i:2 * i + 1, :]).astype(BF16)
        out_ref[:, w:2 * w] = head_norm(seg(base + w, w), g_ref[2 * i + 1:2 * i + 2, :]).astype(BF16)
        out_ref[:, 2 * w:3 * w] = seg(base + 2 * w, w).astype(BF16)
    z_ref[...] = seg(6 * w, D_SSM)
    xbc_ref[...] = seg(6 * w + D_SSM, CONV_DIM)
    dt_ref[...] = seg(6 * w + D_SSM + CONV_DIM, DT_PAD)


def _in_proj(x2d, n1, w_pad, gains, bd, *, tm=1024):
    t = x2d.shape[0]
    row = lambda i: (i, 0)
    resident = lambda shape: pl.BlockSpec(shape, lambda i: (0, 0), pipeline_mode=pl.Buffered(1))
    return pl.pallas_call(
        _in_proj_kernel,
        grid=(t // tm,),
        in_specs=[
            pl.BlockSpec((tm, D_MODEL), row),
            resident((1, D_MODEL)),
            resident((D_MODEL, D_IN_PAD)),
            resident((4, ATTN_WIDTH)),
            resident((ATTN_WIDTH, ATTN_WIDTH)),
        ],
        out_specs=[
            pl.BlockSpec((tm, 3 * ATTN_WIDTH), row),
            pl.BlockSpec((tm, 3 * ATTN_WIDTH), row),
            pl.BlockSpec((tm, D_SSM), row),
            pl.BlockSpec((tm, CONV_DIM), row),
            pl.BlockSpec((tm, DT_PAD), row),
        ],
        out_shape=[
            jax.ShapeDtypeStruct((t, 3 * ATTN_WIDTH), BF16),
            jax.ShapeDtypeStruct((t, 3 * ATTN_WIDTH), BF16),
            jax.ShapeDtypeStruct((t, D_SSM), F32),
            jax.ShapeDtypeStruct((t, CONV_DIM), F32),
            jax.ShapeDtypeStruct((t, DT_PAD), F32),
        ],
        scratch_shapes=[pltpu.VMEM((tm, D_MODEL), BF16)],
        compiler_params=pltpu.CompilerParams(
            dimension_semantics=("parallel",), vmem_limit_bytes=VMEM_LIMIT),
        name="in_proj",
    )(x2d, n1, w_pad, gains, bd)


N_SLABS = 3 * ATTN_WIDTH // LANES
COPY_ROWS = 512
GATHER_UNROLL = 4


def _dilated_kernel(qkv_ref, o_ref, slab_ref, qr_ref, kr_ref, vt_ref, bias_ref, sa_ref, sb_ref,
                    ma_ref, mb_ref, st_ref, y_ref, lse_ref):
    nh = N_ATTN_HEADS
    w = ATTN_WIDTH
    seq = qkv_ref.shape[1]
    n_tiles = seq // BAND
    hm_b = _head_masks(w, BF16)

    @pl.when(pl.program_id(0) == 0)
    def _():
        c = lax.broadcasted_iota(jnp.int32, (2 * BAND, nh * BAND), 0)
        col = lax.broadcasted_iota(jnp.int32, (2 * BAND, nh * BAND), 1)
        a = col % BAND
        head = col // BAND
        slope = jnp.full(col.shape, SLOPES_A[-1], F32)
        for h in range(nh - 2, -1, -1):
            slope = jnp.where(head == h, SLOPES_A[h], slope)
        for p, (_, dil) in enumerate(DILATED_PATTERNS):
            for later, offset in enumerate((a - c, a + BAND - c)):
                valid = (offset >= 0) & (offset <= BAND)
                bias_ref[p, later] = jnp.where(valid, -(slope * (LOG2E * dil)) * offset.astype(F32), NEG_INF)

    def copy(i, carry):
        rows = pl.ds(pl.multiple_of(i * COPY_ROWS, COPY_ROWS), COPY_ROWS)
        x = qkv_ref[0, rows, :].astype(F32)
        for j in range(N_SLABS):
            slab_ref[j, rows, :] = x[:, j * LANES:(j + 1) * LANES]
        return carry
    lax.fori_loop(0, seq // COPY_ROWS, copy, 0)

    def run(p, dil, first_pattern):
        nblk = seq // dil // BAND

        def sub_rows(t):
            if dil == 1:
                return pl.ds(pl.multiple_of(t * BAND, BAND), BAND)
            return pl.ds(t // nblk + dil * BAND * (t % nblk), BAND, stride=dil)

        def gather(i, carry):
            for u in range(GATHER_UNROLL):
                t = i * GATHER_UNROLL + u
                parts = [slab_ref[j, sub_rows(t), :] for j in range(N_SLABS)]
                dst = pl.ds(pl.multiple_of(t * BAND, BAND), BAND)
                qr_ref[dst, :] = jnp.concatenate(parts[0:2], axis=1).astype(BF16)
                kr_ref[dst, :] = jnp.concatenate(parts[2:4], axis=1).astype(BF16)
                vt_ref[t] = _vt_with_ones(jnp.concatenate(parts[4:6], axis=1).T)
            return carry
        lax.fori_loop(0, n_tiles // GATHER_UNROLL, gather, 0)

        def scores(t, buf):
            dst_ref, mx_ref = buf
            tt = jnp.minimum(t, n_tiles - 1)
            later = jnp.minimum(tt % nblk, 1)
            q = qr_ref[pl.ds(pl.multiple_of(tt * BAND, BAND), BAND), :]
            q4 = jnp.concatenate([q * hm_b[h] for h in range(nh)], axis=0)
            k0 = pl.multiple_of((tt - later) * BAND, BAND)
            s = _dot_nt(kr_ref[pl.ds(k0, 2 * BAND), :], q4) + bias_ref[p, later]
            dst_ref[...] = s
            for h in range(nh):
                mx_ref[h:h + 1, :] = jnp.max(s[:, h * BAND:(h + 1) * BAND], axis=0, keepdims=True)

        def softmax_pv(t, buf, slot):
            src_ref, mx_ref = buf
            kb = t - jnp.minimum(t % nblk, 1)
            outs, lses = [], []
            for h in range(nh):
                cs = slice(h * BAND, (h + 1) * BAND)
                hr = slice(h * VT_ROWS, (h + 1) * VT_ROWS)
                mx = mx_ref[h:h + 1, :]
                pexp = jnp.exp2(src_ref[:, cs] - mx)
                vt = jnp.concatenate([vt_ref[kb, hr, :], vt_ref[kb + 1, hr, :]], axis=1)
                pv = _dot(vt, pexp.astype(BF16))
                l = pv[HEAD_DIM:HEAD_DIM + 1, :]
                outs.append(pv[0:HEAD_DIM, :] * (1.0 / l))
                lses.append(jnp.broadcast_to(mx + jnp.log2(l), (HEAD_DIM, BAND)))
            st_ref[slot, 0] = jnp.concatenate(outs, axis=0)
            st_ref[slot, 1] = jnp.concatenate(lses, axis=0)

        def merge(m, slot):
            out = st_ref[slot, 0].T
            lse = st_ref[slot, 1].T
            rows = sub_rows(jnp.maximum(m, 0))
            for sl in range(w // LANES):
                ls = slice(sl * LANES, (sl + 1) * LANES)
                if first_pattern:
                    y_ref[sl, rows, :] = out[:, ls]
                    lse_ref[sl, rows, :] = lse[:, ls]
                else:
                    l_old = lse_ref[sl, rows, :]
                    mx2 = jnp.maximum(l_old, lse[:, ls])
                    e_old = jnp.exp2(l_old - mx2)
                    e_new = jnp.exp2(lse[:, ls] - mx2)
                    den = e_old + e_new
                    y_ref[sl, rows, :] = (y_ref[sl, rows, :] * e_old + out[:, ls] * e_new) / den
                    lse_ref[sl, rows, :] = mx2 + jnp.log2(den)

        buf_a, buf_b = (sa_ref, ma_ref), (sb_ref, mb_ref)
        scores(0, buf_a)
        st_ref[1, 0] = jnp.zeros(st_ref.shape[2:], F32)
        st_ref[1, 1] = jnp.full(st_ref.shape[2:], NEG_INF, F32)

        def pair(i, carry):
            t = 2 * i
            scores(t + 1, buf_b)
            softmax_pv(t, buf_a, 0)
            merge(t - 1, 1)
            scores(t + 2, buf_a)
            softmax_pv(t + 1, buf_b, 1)
            merge(t, 0)
            return carry
        lax.fori_loop(0, n_tiles // 2, pair, 0)
        merge(n_tiles - 1, 1)

    order = sorted(range(len(DILATED_PATTERNS)), key=lambda p: -DILATED_PATTERNS[p][1])
    for p in order:
        run(p, DILATED_PATTERNS[p][1], p == order[0])

    def emit(i, carry):
        rows = pl.ds(pl.multiple_of(i * COPY_ROWS, COPY_ROWS), COPY_ROWS)
        o_ref[0, rows, :] = jnp.concatenate(
            [y_ref[sl, rows, :] for sl in range(w // LANES)], axis=1).astype(BF16)
        return carry
    lax.fori_loop(0, seq // COPY_ROWS, emit, 0)


def _dilated(qkv):
    b, s, _ = qkv.shape
    w = ATTN_WIDTH
    cols = N_ATTN_HEADS * BAND
    return pl.pallas_call(
        _dilated_kernel,
        grid=(b,),
        in_specs=[pl.BlockSpec((1, s, 3 * w), lambda bi: (bi, 0, 0))],
        out_specs=pl.BlockSpec((1, s, w), lambda bi: (bi, 0, 0)),
        out_shape=jax.ShapeDtypeStruct((b, s, w), BF16),
        scratch_shapes=[pltpu.VMEM((N_SLABS, s, LANES), F32),
                        pltpu.VMEM((s, w), BF16),
                        pltpu.VMEM((s, w), BF16),
                        pltpu.VMEM((s // BAND, N_ATTN_HEADS * VT_ROWS, BAND), BF16),
                        pltpu.VMEM((len(DILATED_PATTERNS), 2, 2 * BAND, cols), F32),
                        pltpu.VMEM((2 * BAND, cols), F32),
                        pltpu.VMEM((2 * BAND, cols), F32),
                        pltpu.VMEM((8, BAND), F32),
                        pltpu.VMEM((8, BAND), F32),
                        pltpu.VMEM((2, 2, w, BAND), F32),
                        pltpu.VMEM((w // LANES, s, LANES), F32),
                        pltpu.VMEM((w // LANES, s, LANES), F32)],
        compiler_params=pltpu.CompilerParams(
            dimension_semantics=("arbitrary",), vmem_limit_bytes=VMEM_LIMIT),
        name="dilated",
    )(qkv)


def _softplus(x):
    return jnp.maximum(x, 0.0) + jnp.log1p(jnp.exp(-jnp.abs(x)))


def _silu(x):
    return x / (1.0 + jnp.exp(-x))


def _ssd_kernel(xbc_ref, z_ref, dt_ref, cw_ref, cb_ref, dtb_ref, alog_ref, dsk_ref, nw_ref,
                y_ref, ext_ref, tail_ref, state_ref, *, chunks):
    rows = chunks * SSM_CHUNK
    halo = 8
    cl = SSM_CHUNK
    gw = D_SSM // SSM_GROUPS
    hpg = SSM_HEADS // SSM_GROUPS

    @pl.when(pl.program_id(1) == 0)
    def _():
        tail_ref[...] = jnp.zeros_like(tail_ref)
        state_ref[...] = jnp.zeros_like(state_ref)

    ext_ref[0:halo, :] = tail_ref[...]
    ext_ref[halo:halo + rows, :] = xbc_ref[0]
    tail_ref[...] = xbc_ref[0, rows - halo:rows, :]
    conv = cb_ref[...] + cw_ref[SSM_CONV - 1:SSM_CONV, :] * ext_ref[halo:halo + rows, :]
    for j in range(1, SSM_CONV):
        conv = conv + cw_ref[SSM_CONV - 1 - j:SSM_CONV - j, :] * ext_ref[halo - j:halo - j + rows, :]
    xact = _silu(conv)

    dt =_softplus(dt_ref[0] + dtb_ref[...])
    da = dt * (-jnp.exp(alog_ref[...]))

    ri = lax.broadcasted_iota(jnp.int32, (cl, cl), 0)
    ci = lax.broadcasted_iota(jnp.int32, (cl, cl), 1)
    tri = ri >= ci
    tril = tri.astype(BF16)
    eh = lax.broadcasted_iota(jnp.int32, (DT_PAD, D_SSM), 0)
    el = lax.broadcasted_iota(jnp.int32, (DT_PAD, D_SSM), 1) // HEAD_DIM
    expand = (eh == el).astype(BF16)
    hm_g = _head_masks(gw, F32)

    for c in range(chunks):
        r0 = c * cl
        xs = xact[r0:r0 + cl, 0:D_SSM]
        bm = xact[r0:r0 + cl, D_SSM:D_SSM + SSM_GROUPS * SSM_STATE]
        cm = xact[r0:r0 + cl, D_SSM + SSM_GROUPS * SSM_STATE:]
        acs = _dot_f32_rhs(tril, da[r0:r0 + cl])
        acs_t = acs.T
        acs_full = _dot_f32_lhs(acs, expand)
        dt_full = _dot_f32_lhs(dt[r0:r0 + cl], expand)
        xd = xs * dt_full
        last = acs_full[cl - 1:cl, :]
        xdd = xd * jnp.exp(last - acs_full)
        exp_acs = jnp.exp(acs_full)
        chunk_decay = jnp.exp(last)
        y_parts = []
        for g in range(SSM_GROUPS):
            gs = slice(g * gw, (g + 1) * gw)
            bg = bm[:, g * SSM_STATE:(g + 1) * SSM_STATE].astype(BF16)
            cg = cm[:, g * SSM_STATE:(g + 1) * SSM_STATE].astype(BF16)
            cb = _dot_nt(cg, bg)
            s_in = state_ref[:, gs]
            y_g = _dot(cg, s_in.astype(BF16)) * exp_acs[:, gs]
            state_ref[:, gs] = s_in * chunk_decay[:, gs] + _dot_tn(bg, xdd[:, gs].astype(BF16))
            xd_g = xd[:, gs]
            for hh in range(hpg):
                h = g * hpg + hh
                diff = acs[:, h:h + 1] - acs_t[h:h + 1, :]
                lmat = jnp.where(tri, jnp.exp(jnp.where(tri, diff, 0.0)), 0.0)
                y_g = y_g + _dot((cb * lmat).astype(BF16), (xd_g * hm_g[hh]).astype(BF16))
            y_parts.append(y_g)
        y = jnp.concatenate(y_parts, axis=1) + dsk_ref[...] * xs
        y = y * _silu(z_ref[0, r0:r0 + cl, :])
        outs = []
        for g in range(SSM_GROUPS):
            yg = y[:, g * gw:(g + 1) * gw]
            ms = jnp.mean(yg * yg, axis=-1, keepdims=True)
            outs.append(yg * lax.rsqrt(ms + EPS) * nw_ref[:, g * gw:(g + 1) * gw])
        y_ref[0, r0:r0 + cl, :] = jnp.concatenate(outs, axis=1).astype(BF16)


def _ssd(xbc, z, dt, conv_w, conv_b, dt_bias, a_log, d_skip_full, norm_w, *, chunks=4):
    b, s, _ = xbc.shape
    rows = chunks * SSM_CHUNK
    blk = lambda width: pl.BlockSpec((1, rows, width), lambda bi, ci: (bi, ci, 0))
    const = lambda shape: pl.BlockSpec(shape, lambda bi, ci: (0, 0))
    return pl.pallas_call(
        functools.partial(_ssd_kernel, chunks=chunks),
        grid=(b, s // rows),
        in_specs=[blk(CONV_DIM), blk(D_SSM), blk(DT_PAD),
                  const((SSM_CONV, CONV_DIM)), const((1, CONV_DIM)), const((1, DT_PAD)),
                  const((1, DT_PAD)), const((1, D_SSM)), const((1, D_SSM))],
        out_specs=blk(D_SSM),
        out_shape=jax.ShapeDtypeStruct((b, s, D_SSM), BF16),
        scratch_shapes=[pltpu.VMEM((rows + 8, CONV_DIM), F32),
                        pltpu.VMEM((8, CONV_DIM), F32),
                        pltpu.VMEM((SSM_STATE, D_SSM), F32)],
        compiler_params=pltpu.CompilerParams(
            dimension_semantics=("parallel", "arbitrary"), vmem_limit_bytes=VMEM_LIMIT),
        name="ssd",
    )(xbc, z, dt, conv_w, conv_b, dt_bias, a_log, d_skip_full, norm_w)


MOBA_QT = LANES


def _moba_kernel(q_ref, k_ref, v_ref, o_ref, kmean_ref, vt_ref, rel_ref, relown_ref, q4t_ref,
                 bias_ref, so_ref, sa_ref, sb_ref, mo_ref, ma_ref, mb_ref, acc_ref, m_ref, l_ref,
                 *, n_blk):
    nh = N_ATTN_HEADS
    bs = MOBA_BLOCK
    cols = nh * bs
    qb = pl.program_id(1)

    col = lax.broadcasted_iota(jnp.int32, (1, cols), 1)
    col_head = col // bs
    slope = jnp.full((1, cols), SLOPES_C[-1], F32)
    for h in range(nh - 2, -1, -1):
        slope = jnp.where(col_head == h, SLOPES_C[h], slope)

    @pl.when(qb == 0)
    def _():
        for n in range(n_blk):
            kblk = k_ref[0, n * bs:(n + 1) * bs, :].astype(F32)
            kmean_ref[n:n + 1, :] = jnp.sum(kblk, axis=0, keepdims=True) * (1.0 / bs)
            vt_ref[n] = _vt_with_ones(v_ref[0, n * bs:(n + 1) * bs, :].astype(F32).T)
        c = lax.broadcasted_iota(jnp.int32, (bs, cols), 0)
        a = lax.broadcasted_iota(jnp.int32, (bs, cols), 1) % bs
        rel = -(slope * LOG2E) * (a - c).astype(F32)
        rel_ref[...] = rel
        relown_ref[...] = jnp.where(a >= c, rel, NEG_INF)

    qt = q_ref[0].astype(F32).T
    frow = lax.broadcasted_iota(jnp.int32, (ATTN_WIDTH, bs), 0) // HEAD_DIM
    for h in range(nh):
        q4t_ref[:, h * bs:(h + 1) * bs] = jnp.where(frow == h, qt, 0.0).astype(BF16)

    def block_gate():
        km = kmean_ref[...]
        km_hi = km.astype(BF16)
        km_lo = (km - km_hi.astype(F32)).astype(BF16)
        q4t = q4t_ref[...]
        return _dot(km_hi, q4t) + _dot(km_lo, q4t)

    def select_blocks(gate):
        blk =lax.broadcasted_iota(jnp.int32, (n_blk, cols), 0)
        blk_f = blk.astype(F32)
        past = blk < qb
        g = jnp.where(past, gate, NEG_INF)
        sel = jnp.zeros((n_blk, cols), jnp.bool_)
        for _ in range(MOBA_TOPK):
            mx = jnp.max(g, axis=0, keepdims=True)
            first = jnp.min(jnp.where(g == mx, blk_f, float(n_blk)), axis=0, keepdims=True)
            pick = blk_f == first
            sel = sel | pick
            g = jnp.where(pick, -jnp.inf, g)
        sel = sel & past
        bias_ref[...] = jnp.where(sel, -(slope * (LOG2E * bs)) * (qb - blk).astype(F32), NEG_INF)

    def scores(kb, buf, slot, rel_mat_ref):
        dst_ref, mx_ref = buf
        kb = jnp.minimum(kb, n_blk - 1)
        kblk = k_ref[0, pl.ds(pl.multiple_of(kb * bs, bs), bs), :]
        for h in range(nh):
            hs = slice(h * bs, (h + 1) * bs)
            s = _dot(kblk, q4t_ref[:, hs]) + rel_mat_ref[:, hs]
            dst_ref[slot, :, hs] = s
            for i in range(bs // MOBA_QT):
                t = h * bs // MOBA_QT + i
                mx_ref[slot, t:t + 1, :] = jnp.max(s[:, i * MOBA_QT:(i + 1) * MOBA_QT], axis=0, keepdims=True)

    def attend(kb, buf, bias_rows, slot=0):
        src_ref, mx_ref = buf
        first = bias_rows is None
        n = 1 if first else len(bias_rows)
        for h in range(nh):
            hs = slice(h * bs, (h + 1) * bs)
            ps, alphas = [[] for _ in range(n)], []
            tiles = range(h * bs // MOBA_QT, (h + 1) * bs // MOBA_QT)
            for t in tiles:
                cs = slice(t * MOBA_QT, (t + 1) * MOBA_QT)
                mxs = [mx_ref[slot + i, t:t + 1, :] for i in range(n)]
                if first:
                    m_new = mxs[0]
                    shifts = [m_new]
                else:
                    bs_t = [b[:, cs] for b in bias_rows]
                    m_old = m_ref[t:t + 1, :]
                    m_new = m_old
                    for mx, b in zip(mxs, bs_t):
                        m_new = jnp.maximum(m_new, mx + b)
                    shifts = [m_new - b for b in bs_t]
                    alphas.append(jnp.exp2(m_old - m_new))
                m_ref[t:t + 1, :] = m_new
                for i in range(n):
                    ps[i].append(jnp.exp2(src_ref[slot + i, :, cs] - shifts[i]).astype(BF16))
            p_all = jnp.concatenate([jnp.concatenate(p, axis=1) for p in ps], axis=0)
            vt = jnp.concatenate([vt_ref[kb + i, h * VT_ROWS:(h + 1) * VT_ROWS, :] for i in range(n)], axis=1)
            pv = _dot(vt, p_all)
            for i, t in enumerate(tiles):
                lsum = pv[HEAD_DIM:HEAD_DIM + 1, i * MOBA_QT:(i + 1) * MOBA_QT]
                l_ref[t:t + 1, :] = lsum if first else alphas[i] * l_ref[t:t + 1, :] + lsum
            pv = pv[0:HEAD_DIM, :]
            acc_ref[:, hs] = pv if first else jnp.concatenate(alphas, axis=1) * acc_ref[:, hs] + pv

    buf_own, buf_a, buf_b = (so_ref, mo_ref), (sa_ref, ma_ref), (sb_ref, mb_ref)
    gate = block_gate()
    scores(qb, buf_own, 0, relown_ref)
    scores(0, buf_a, 0, rel_ref)
    scores(1, buf_a, 1, rel_ref)
    attend(qb, buf_own, None)
    select_blocks(gate)

    def body(j, carry):
        kb = 2 * j

        def step(cur, nxt):
            scores(kb + 2, nxt, 0, rel_ref)
            scores(kb + 3, nxt, 1, rel_ref)
            attend(kb, cur, [bias_ref[pl.ds(kb, 1), :]], slot=0)
            attend(kb + 1, cur, [bias_ref[pl.ds(kb + 1, 1), :]], slot=1)

        @pl.when(j % 2 == 0)
        def _():
            step(buf_a, buf_b)

        @pl.when(j % 2 == 1)
        def _():
            step(buf_b, buf_a)

        return carry

    n_pairs = qb // 2
    lax.fori_loop(0, n_pairs, body, 0)

    for parity, cur in ((0, buf_a), (1, buf_b)):
        @pl.when((qb % 2 == 1) & (n_pairs % 2 == parity))
        def _(cur=cur):
            attend(qb - 1, cur, [bias_ref[pl.ds(qb - 1, 1), :]])

    inv = 1.0 / l_ref[...]
    heads = []
    for h in range(nh):
        tiles = range(h * bs // MOBA_QT, (h + 1) * bs // MOBA_QT)
        heads.append(jnp.concatenate(
            [acc_ref[:, t * MOBA_QT:(t + 1) * MOBA_QT] * inv[t:t + 1, :] for t in tiles], axis=1))
    o_ref[0] = jnp.concatenate(heads, axis=0).T.astype(BF16)


def _moba(qkv):
    b, s, _ = qkv.shape
    w = ATTN_WIDTH
    bs = MOBA_BLOCK
    n_blk = s // bs
    cols = N_ATTN_HEADS * bs
    return pl.pallas_call(
        functools.partial(_moba_kernel, n_blk=n_blk),
        grid=(b, n_blk),
        in_specs=[pl.BlockSpec((1, bs, w), lambda bi, qi: (bi, qi, 0)),
                  pl.BlockSpec((1, s, w), lambda bi, qi: (bi, 0, 1)),
                  pl.BlockSpec((1, s, w), lambda bi, qi: (bi, 0, 2))],
        out_specs=pl.BlockSpec((1, bs, w), lambda bi, qi: (bi, qi, 0)),
        out_shape=jax.ShapeDtypeStruct((b, s, w), BF16),
        scratch_shapes=[pltpu.VMEM((n_blk, w), F32),
                        pltpu.VMEM((n_blk, N_ATTN_HEADS * VT_ROWS, bs), BF16),
                        pltpu.VMEM((bs, cols), F32),
                        pltpu.VMEM((bs, cols), F32),
                        pltpu.VMEM((w, cols), BF16),
                        pltpu.VMEM((n_blk, cols), F32),
                        pltpu.VMEM((1, bs, cols), F32),
                        pltpu.VMEM((2, bs, cols), F32),
                        pltpu.VMEM((2, bs, cols), F32),
                        pltpu.VMEM((1, cols // MOBA_QT, MOBA_QT), F32),
                        pltpu.VMEM((2, cols // MOBA_QT, MOBA_QT), F32),
                        pltpu.VMEM((2, cols // MOBA_QT, MOBA_QT), F32),
                        pltpu.VMEM((HEAD_DIM, cols), F32),
                        pltpu.VMEM((cols // MOBA_QT, MOBA_QT), F32),
                        pltpu.VMEM((cols // MOBA_QT, MOBA_QT), F32)],
        compiler_params=pltpu.CompilerParams(
            dimension_semantics=("parallel", "arbitrary"), vmem_limit_bytes=VMEM_LIMIT),
        name="moba",
    )(qkv, qkv, qkv)


def _out_mlp_kernel(x_ref, ya_ref, yb_ref, yc_ref, wo_ref, n2_ref, w1_ref, w2_ref,
                    out_ref, h_ref, *, tf):
    tm = x_ref.shape[0]
    pr = tm // ROW_PARTS
    for s in range(ROW_PARTS):
        rows = slice(s * pr, (s + 1) * pr)
        y = jnp.concatenate([ya_ref[rows, :], yb_ref[rows, :], yc_ref[rows, :]], axis=1)
        x1 = x_ref[rows, :] + _dot(y, wo_ref[...])
        out_ref[rows, :] = x1
        ms = jnp.mean(x1 * x1, axis=-1, keepdims=True)
        h_ref[rows, :] = (x1 * lax.rsqrt(ms + EPS) * n2_ref[...]).astype(BF16)
    for c in range(D_FF // tf):
        cols = slice(c * tf, (c + 1) * tf)
        u = jnp.maximum(_dot(h_ref[...], w1_ref[:, cols]), 0.0)
        out_ref[...] += _dot((u * u).astype(BF16), w2_ref[cols, :])


def _out_mlp(x2d, ya, yb, yc, w_out, n2, w1, w2, *, tm=1024, tf=1024):
    t = x2d.shape[0]
    row = lambda width: pl.BlockSpec((tm, width), lambda i: (i, 0))
    resident = lambda shape: pl.BlockSpec(shape, lambda i: (0, 0), pipeline_mode=pl.Buffered(1))
    return pl.pallas_call(
        functools.partial(_out_mlp_kernel, tf=tf),
        grid=(t // tm,),
        in_specs=[row(D_MODEL), row(ATTN_WIDTH), row(D_SSM), row(ATTN_WIDTH),
                  resident((D_MODEL, D_MODEL)), resident((1, D_MODEL)),
                  resident((D_MODEL, D_FF)), resident((D_FF, D_MODEL))],
        out_specs=row(D_MODEL),
        out_shape=jax.ShapeDtypeStruct((t, D_MODEL), F32),
        scratch_shapes=[pltpu.VMEM((tm, D_MODEL), BF16)],
        compiler_params=pltpu.CompilerParams(
            dimension_semantics=("parallel",), vmem_limit_bytes=VMEM_LIMIT),
        name="out_mlp",
    )(x2d, ya, yb, yc, w_out, n2, w1, w2)


def _layer(x2d, bsz, seq, norm1_w, w_in, a_q_norm, a_k_norm, c_q_norm, c_k_norm, conv_w, conv_b,
           dt_bias, a_log, d_skip, ssm_norm_w, w_out, norm2_w, w_mlp_in, w_mlp_out):
    w_pad = jnp.pad(w_in, ((0, 0), (0, DT_PAD - SSM_HEADS))).astype(BF16)
    scale = HEAD_DIM ** -0.5 * LOG2E
    gains = jnp.stack([jnp.tile(a_q_norm, N_ATTN_HEADS) * scale, jnp.tile(a_k_norm, N_ATTN_HEADS),
                       jnp.tile(c_q_norm, N_ATTN_HEADS) * scale, jnp.tile(c_k_norm, N_ATTN_HEADS)])
    lane = jnp.arange(ATTN_WIDTH) // HEAD_DIM
    bd = (lane[:, None] == lane[None, :]).astype(BF16)
    pad_h = lambda v: jnp.pad(v, (0, DT_PAD - SSM_HEADS)).reshape(1, DT_PAD)

    qkva, qkvc, z, xbc, dt = _in_proj(x2d, norm1_w.reshape(1, -1), w_pad, gains, bd)

    ya = _dilated(qkva.reshape(bsz, seq, -1))
    yb =_ssd(xbc.reshape(bsz, seq, -1), z.reshape(bsz, seq, -1), dt.reshape(bsz, seq, -1),
              conv_w, conv_b.reshape(1, -1), pad_h(dt_bias), pad_h(a_log),
              jnp.repeat(d_skip, HEAD_DIM).reshape(1, -1), ssm_norm_w.reshape(1, -1))
    yc = _moba(qkvc.reshape(bsz, seq, -1))

    return _out_mlp(x2d, ya.reshape(bsz * seq, -1), yb.reshape(bsz * seq, -1), yc.reshape(bsz * seq, -1),
                    w_out.astype(BF16), norm2_w.reshape(1, -1),
                    w_mlp_in.astype(BF16), w_mlp_out.astype(BF16))


def kernel(x, norm1_w, w_in, a_q_norm, a_k_norm, c_q_norm, c_k_norm, conv_w, conv_b, dt_bias,
           a_log, d_skip, ssm_norm_w, w_out, norm2_w, w_mlp_in, w_mlp_out):
    bsz, seq, d = x.shape
    x2d = x.reshape(bsz * seq, d)
    for i in range(norm1_w.shape[0]):
        x2d = _layer(x2d, bsz, seq, norm1_w[i], w_in[i], a_q_norm[i], a_k_norm[i], c_q_norm[i],
                     c_k_norm[i], conv_w[i], conv_b[i], dt_bias[i], a_log[i], d_skip[i],
                     ssm_norm_w[i], w_out[i], norm2_w[i], w_mlp_in[i], w_mlp_out[i])
    return x2d.reshape(bsz, seq, d)
```

```python
import functools

import jax
import jax.numpy as jnp
from jax import lax
from jax.experimental import pallas as pl
from jax.experimental.pallas import tpu as pltpu

F32 = jnp.float32
BF16 = jnp.bfloat16

D_MODEL = 1024
HEAD_DIM = 64
N_ATTN_HEADS = 4
ATTN_WIDTH = N_ATTN_HEADS * HEAD_DIM
DILATED_PATTERNS = ((128, 1), (512, 4), (2048, 16))
BAND = 128
MOBA_BLOCK = 256
MOBA_TOPK = 3
SSM_HEADS = 8
D_SSM = SSM_HEADS * HEAD_DIM
SSM_STATE = 128
SSM_GROUPS = 2
SSM_CONV = 4
SSM_CHUNK = 128
CONV_DIM = D_SSM + 2 * SSM_GROUPS * SSM_STATE
D_FF = 4 * D_MODEL
D_IN = 2 * 3 * ATTN_WIDTH + D_SSM + CONV_DIM + SSM_HEADS
LANES = 128
DT_PAD = LANES
D_IN_PAD = D_IN - SSM_HEADS + DT_PAD
EPS = 1e-6
NEG_INF = -1e30
LOG2E = 1.4426950408889634
SLOPES_A = tuple(2.0 ** -(2 * h + 1) for h in range(N_ATTN_HEADS))
SLOPES_C = tuple(2.0 ** -(2 * h + 2) for h in range(N_ATTN_HEADS))

VMEM_LIMIT = 56 * 1024 * 1024
ROW_PARTS = 4


def _split3(a):
    hi = a.astype(BF16)
    r1 = a - hi.astype(F32)
    mid = r1.astype(BF16)
    lo = (r1 - mid.astype(F32)).astype(BF16)
    return hi, mid, lo


def _dot(a, b):
    return jnp.dot(a, b, preferred_element_type=F32)


def _dot_nt(a, b):
    return lax.dot_general(a, b, (((1,), (1,)), ((), ())), preferred_element_type=F32)


def _dot_tn(a, b):
    return lax.dot_general(a, b, (((0,), (0,)), ((), ())), preferred_element_type=F32)


def _dot_f32_lhs(a, sel):
    hi, mid, lo = _split3(a)
    return _dot(hi, sel) + _dot(mid, sel) + _dot(lo, sel)


def _dot_f32_rhs(sel, b):
    hi, mid, lo = _split3(b)
    return _dot(sel, hi) + _dot(sel, mid) + _dot(sel, lo)


VT_ROWS = HEAD_DIM + 16


def _vt_with_ones(vt):
    keys = vt.shape[1]
    ones_tile = (lax.broadcasted_iota(jnp.int32, (VT_ROWS - HEAD_DIM, keys), 0) == 0).astype(F32)
    parts = []
    for h in range(vt.shape[0] // HEAD_DIM):
        parts += [vt[h * HEAD_DIM:(h + 1) * HEAD_DIM, :], ones_tile]
    return jnp.concatenate(parts, axis=0).astype(BF16)


def _head_masks(width, dtype):
    lane = lax.broadcasted_iota(jnp.int32, (1, width), 1)
    return [(lane // HEAD_DIM == h).astype(dtype) for h in range(width // HEAD_DIM)]


def _row_slopes(rows_per_head, slopes):
    row = lax.broadcasted_iota(jnp.int32, (len(slopes) * rows_per_head, 1), 0) // rows_per_head
    out = jnp.full(row.shape, slopes[-1], F32)
    for h in range(len(slopes) - 2, -1, -1):
        out = jnp.where(row == h, slopes[h], out)
    return out


def _in_proj_kernel(x_ref, n1_ref, w_ref, g_ref, bd_ref,
                    qkva_ref, qkvc_ref, z_ref, xbc_ref, dt_ref, h_ref):
    pr = x_ref.shape[0] // ROW_PARTS
    for s in range(ROW_PARTS):
        rows = slice(s * pr, (s + 1) * pr)
        x = x_ref[rows, :]
        ms = jnp.mean(x * x, axis=-1, keepdims=True)
        h_ref[rows, :] = (x * lax.rsqrt(ms + EPS) * n1_ref[...]).astype(BF16)
    bd = bd_ref[...]

    def seg(lo, width):
        return _dot(h_ref[...], w_ref[:, lo:lo + width])

    def head_norm(y, g):
        ss = _dot((y * y).astype(BF16), bd)
        return y * lax.rsqrt(ss * (1.0 / HEAD_DIM) + EPS) * g

    w = ATTN_WIDTH
    for i, out_ref in enumerate((qkva_ref, qkvc_ref)):
        base = 3 * w * i
        out_ref[:, 0:w] = head_norm(seg(base, w), g_ref[2 * i:2 * i + 1, :]).astype(BF16)
        out_ref[:, w:2 * w] = head_norm(seg(base + w, w), g_ref[2 * i + 1:2 * i + 2, :]).astype(BF16)
        out_ref[:, 2 * w:3 * w] = seg(base + 2 * w, w).astype(BF16)
    z_ref[...] = seg(6 * w, D_SSM)
    xbc_ref[...] = seg(6 * w + D_SSM, CONV_DIM)
    dt_ref[...] = seg(6 * w + D_SSM + CONV_DIM, DT_PAD)


def _in_proj(x2d, n1, w_pad, gains, bd, *, tm=1024):
    t = x2d.shape[0]
    row = lambda i: (i, 0)
    resident = lambda shape: pl.BlockSpec(shape, lambda i: (0, 0), pipeline_mode=pl.Buffered(1))
    return pl.pallas_call(
        _in_proj_kernel,
        grid=(t // tm,),
        in_specs=[
            pl.BlockSpec((tm, D_MODEL), row),
            resident((1, D_MODEL)),
            resident((D_MODEL, D_IN_PAD)),
            resident((4, ATTN_WIDTH)),
            resident((ATTN_WIDTH, ATTN_WIDTH)),
        ],
        out_specs=[
            pl.BlockSpec((tm, 3 * ATTN_WIDTH), row),
            pl.BlockSpec((tm, 3 * ATTN_WIDTH), row),
            pl.BlockSpec((tm, D_SSM), row),
            pl.BlockSpec((tm, CONV_DIM), row),
            pl.BlockSpec((tm, DT_PAD), row),
        ],
        out_shape=[
            jax.ShapeDtypeStruct((t, 3 * ATTN_WIDTH), BF16),
            jax.ShapeDtypeStruct((t, 3 * ATTN_WIDTH), BF16),
            jax.ShapeDtypeStruct((t, D_SSM), F32),
            jax.ShapeDtypeStruct((t, CONV_DIM), F32),
            jax.ShapeDtypeStruct((t, DT_PAD), F32),
        ],
        scratch_shapes=[pltpu.VMEM((tm, D_MODEL), BF16)],
        compiler_params=pltpu.CompilerParams(
            dimension_semantics=("parallel",), vmem_limit_bytes=VMEM_LIMIT),
        name="in_proj",
    )(x2d, n1, w_pad, gains, bd)


N_SLABS = 3 * ATTN_WIDTH // LANES
COPY_ROWS = 512
GATHER_UNROLL = 4


def _dilated_kernel(qkv_ref, o_ref, slab_ref, qr_ref, kr_ref, vt_ref, bias_ref, sa_ref, sb_ref,
                    st_ref, y_ref, lse_ref):
    nh = N_ATTN_HEADS
    w = ATTN_WIDTH
    seq = qkv_ref.shape[1]
    n_tiles = seq // BAND
    hm_b = _head_masks(w, BF16)

    @pl.when(pl.program_id(0) == 0)
    def _():
        c = lax.broadcasted_iota(jnp.int32, (2 * BAND, nh * BAND), 0)
        col = lax.broadcasted_iota(jnp.int32, (2 * BAND, nh * BAND), 1)
        a = col % BAND
        head = col // BAND
        slope = jnp.full(col.shape, SLOPES_A[-1], F32)
        for h in range(nh - 2, -1, -1):
            slope = jnp.where(head == h, SLOPES_A[h], slope)
        for p, (_, dil) in enumerate(DILATED_PATTERNS):
            for later, offset in enumerate((a - c, a + BAND - c)):
                valid = (offset >= 0) & (offset <= BAND)
                bias_ref[p, later] = jnp.where(valid, -(slope * (LOG2E * dil)) * offset.astype(F32), NEG_INF)

    def copy(i, carry):
        rows = pl.ds(pl.multiple_of(i * COPY_ROWS, COPY_ROWS), COPY_ROWS)
        x = qkv_ref[0, rows, :].astype(F32)
        for j in range(N_SLABS):
            slab_ref[j, rows, :] = x[:, j * LANES:(j + 1) * LANES]
        return carry
    lax.fori_loop(0, seq // COPY_ROWS, copy, 0)

    def run(p, dil, first_pattern):
        nblk = seq // dil // BAND

        def sub_rows(t):
            if dil == 1:
                return pl.ds(pl.multiple_of(t * BAND, BAND), BAND)
            return pl.ds(t // nblk + dil * BAND * (t % nblk), BAND, stride=dil)

        def gather(i, carry):
            for u in range(GATHER_UNROLL):
                t = i * GATHER_UNROLL + u
                parts = [slab_ref[j, sub_rows(t), :] for j in range(N_SLABS)]
                dst = pl.ds(pl.multiple_of(t * BAND, BAND), BAND)
                qr_ref[dst, :] = jnp.concatenate(parts[0:2], axis=1).astype(BF16)
                kr_ref[dst, :] = jnp.concatenate(parts[2:4], axis=1).astype(BF16)
                vt_ref[t] = _vt_with_ones(jnp.concatenate(parts[4:6], axis=1).T)
            return carry
        lax.fori_loop(0, n_tiles // GATHER_UNROLL, gather, 0)

        def scores(t, dst_ref):
            tt = jnp.minimum(t, n_tiles - 1)
            later = jnp.minimum(tt % nblk, 1)
            q = qr_ref[pl.ds(pl.multiple_of(tt * BAND, BAND), BAND), :]
            q4 = jnp.concatenate([q * hm_b[h] for h in range(nh)], axis=0)
            k0 = pl.multiple_of((tt - later) * BAND, BAND)
            dst_ref[...] = _dot_nt(kr_ref[pl.ds(k0, 2 * BAND), :], q4) + bias_ref[p, later]

        n_slabs_out = w // LANES
        heads_per_slab = nh // n_slabs_out

        def softmax_pv(t, src_ref, slot, sl):
            kb = t - jnp.minimum(t % nblk, 1)
            for h in range(sl * heads_per_slab, (sl + 1) * heads_per_slab):
                cs = slice(h * BAND, (h + 1) * BAND)
                hr = slice(h * VT_ROWS, (h + 1) * VT_ROWS)
                fr = slice(h * HEAD_DIM, (h + 1) * HEAD_DIM)
                mx = jnp.max(src_ref[:, cs], axis=0, keepdims=True)
                pexp = jnp.exp2(src_ref[:, cs] - mx)
                vt = jnp.concatenate([vt_ref[kb, hr, :], vt_ref[kb + 1, hr, :]], axis=1)
                pv = _dot(vt, pexp.astype(BF16))
                l = pv[HEAD_DIM:HEAD_DIM + 1, :]
                st_ref[slot, 0, fr, :] = pv[0:HEAD_DIM, :] * (1.0 / l)
                st_ref[slot, 1, fr, :] = jnp.broadcast_to(mx + jnp.log2(l), (HEAD_DIM, BAND))

        def merge(m, slot, sl):
            fr = slice(sl * LANES, (sl + 1) * LANES)
            out = st_ref[slot, 0, fr, :].T
            lse = st_ref[slot, 1, fr, :].T
            rows = sub_rows(jnp.maximum(m, 0))
            if first_pattern:
                y_ref[sl, rows, :] = out
                lse_ref[sl, rows, :] = lse
            else:
                l_old = lse_ref[sl, rows, :]
                mx2 = jnp.maximum(l_old, lse)
                e_old = jnp.exp2(l_old - mx2)
                e_new = jnp.exp2(lse - mx2)
                den = e_old + e_new
                y_ref[sl, rows, :] = (y_ref[sl, rows, :] * e_old + out * e_new) / den
                lse_ref[sl, rows, :] = mx2 + jnp.log2(den)

        scores(0, sa_ref)
        st_ref[1, 0] = jnp.zeros(st_ref.shape[2:], F32)
        st_ref[1, 1] = jnp.full(st_ref.shape[2:], NEG_INF, F32)

        def pair(i, carry):
            t = 2 * i
            scores(t + 1, sb_ref)
            for sl in range(n_slabs_out):
                softmax_pv(t, sa_ref, 0, sl)
                merge(t - 1, 1, sl)
            scores(t + 2, sa_ref)
            for sl in range(n_slabs_out):
                softmax_pv(t + 1, sb_ref, 1, sl)
                merge(t, 0, sl)
            return carry
        lax.fori_loop(0, n_tiles // 2, pair, 0)
        for sl in range(n_slabs_out):
            merge(n_tiles - 1, 1, sl)

    order = sorted(range(len(DILATED_PATTERNS)), key=lambda p: -DILATED_PATTERNS[p][1])
    for p in order:
        run(p, DILATED_PATTERNS[p][1], p == order[0])

    def emit(i, carry):
        rows = pl.ds(pl.multiple_of(i * COPY_ROWS, COPY_ROWS), COPY_ROWS)
        o_ref[0, rows, :] = jnp.concatenate(
            [y_ref[sl, rows, :] for sl in range(w // LANES)], axis=1).astype(BF16)
        return carry
    lax.fori_loop(0, seq // COPY_ROWS, emit, 0)


def _dilated(qkv):
    b, s, _ = qkv.shape
    w = ATTN_WIDTH
    cols = N_ATTN_HEADS * BAND
    return pl.pallas_call(
        _dilated_kernel,
        grid=(b,),
        in_specs=[pl.BlockSpec((1, s, 3 * w), lambda bi: (bi, 0, 0))],
        out_specs=pl.BlockSpec((1, s, w), lambda bi: (bi, 0, 0)),
        out_shape=jax.ShapeDtypeStruct((b, s, w), BF16),
        scratch_shapes=[pltpu.VMEM((N_SLABS, s, LANES), F32),
                        pltpu.VMEM((s, w), BF16),
                        pltpu.VMEM((s, w), BF16),
                        pltpu.VMEM((s // BAND, N_ATTN_HEADS * VT_ROWS, BAND), BF16),
                        pltpu.VMEM((len(DILATED_PATTERNS), 2, 2 * BAND, cols), F32),
                        pltpu.VMEM((2 * BAND, cols), F32),
                        pltpu.VMEM((2 * BAND, cols), F32),
                        pltpu.VMEM((2, 2, w, BAND), F32),
                        pltpu.VMEM((w // LANES, s, LANES), F32),
                        pltpu.VMEM((w // LANES, s, LANES), F32)],
        compiler_params=pltpu.CompilerParams(
            dimension_semantics=("arbitrary",), vmem_limit_bytes=VMEM_LIMIT),
        name="dilated",
    )(qkv)


def _softplus(x):
    return jnp.maximum(x, 0.0) + jnp.log1p(jnp.exp(-jnp.abs(x)))


def _silu(x):
    return x / (1.0 + jnp.exp(-x))


def _ssd_kernel(xbc_ref, z_ref, dt_ref, cw_ref, cb_ref, dtb_ref, alog_ref, dsk_ref, nw_ref,
                y_ref, ext_ref, tail_ref, state_ref, *, chunks):
    rows = chunks * SSM_CHUNK
    halo = 8
    cl = SSM_CHUNK
    gw = D_SSM // SSM_GROUPS
    hpg = SSM_HEADS // SSM_GROUPS

    @pl.when(pl.program_id(1) == 0)
    def _():
        tail_ref[...] = jnp.zeros_like(tail_ref)
        state_ref[...] = jnp.zeros_like(state_ref)

    ext_ref[0:halo, :] = tail_ref[...]
    ext_ref[halo:halo + rows, :] = xbc_ref[0]
    tail_ref[...] = xbc_ref[0, rows - halo:rows, :]
    conv = cb_ref[...] + cw_ref[SSM_CONV - 1:SSM_CONV, :] * ext_ref[halo:halo + rows, :]
    for j in range(1, SSM_CONV):
        conv = conv + cw_ref[SSM_CONV - 1 - j:SSM_CONV - j, :] * ext_ref[halo - j:halo - j + rows, :]
    xact = _silu(conv)

    dt =_softplus(dt_ref[0] + dtb_ref[...])
    da = dt * (-jnp.exp(alog_ref[...]))

    ri = lax.broadcasted_iota(jnp.int32, (cl, cl), 0)
    ci = lax.broadcasted_iota(jnp.int32, (cl, cl), 1)
    tri = ri >= ci
    tril = tri.astype(BF16)
    eh = lax.broadcasted_iota(jnp.int32, (DT_PAD, D_SSM), 0)
    el = lax.broadcasted_iota(jnp.int32, (DT_PAD, D_SSM), 1) // HEAD_DIM
    expand = (eh == el).astype(BF16)
    hm_g = _head_masks(gw, F32)

    for c in range(chunks):
        r0 = c * cl
        xs = xact[r0:r0 + cl, 0:D_SSM]
        bm = xact[r0:r0 + cl, D_SSM:D_SSM + SSM_GROUPS * SSM_STATE]
        cm = xact[r0:r0 + cl, D_SSM + SSM_GROUPS * SSM_STATE:]
        acs = _dot_f32_rhs(tril, da[r0:r0 + cl])
        acs_t = acs.T
        acs_full = _dot_f32_lhs(acs, expand)
        dt_full = _dot_f32_lhs(dt[r0:r0 + cl], expand)
        xd = xs * dt_full
        last = acs_full[cl - 1:cl, :]
        xdd = xd * jnp.exp(last - acs_full)
        exp_acs = jnp.exp(acs_full)
        chunk_decay = jnp.exp(last)
        y_parts = []
        for g in range(SSM_GROUPS):
            gs = slice(g * gw, (g + 1) * gw)
            bg = bm[:, g * SSM_STATE:(g + 1) * SSM_STATE].astype(BF16)
            cg = cm[:, g * SSM_STATE:(g + 1) * SSM_STATE].astype(BF16)
            cb = _dot_nt(cg, bg)
            s_in = state_ref[:, gs]
            y_g = _dot(cg, s_in.astype(BF16)) * exp_acs[:, gs]
            state_ref[:, gs] = s_in * chunk_decay[:, gs] + _dot_tn(bg, xdd[:, gs].astype(BF16))
            xd_g = xd[:, gs]
            for hh in range(hpg):
                h = g * hpg + hh
                diff = acs[:, h:h + 1] - acs_t[h:h + 1, :]
                lmat = jnp.where(tri, jnp.exp(jnp.where(tri, diff, 0.0)), 0.0)
                y_g = y_g + _dot((cb * lmat).astype(BF16), (xd_g * hm_g[hh]).astype(BF16))
            y_parts.append(y_g)
        y = jnp.concatenate(y_parts, axis=1) + dsk_ref[...] * xs
        y = y * _silu(z_ref[0, r0:r0 + cl, :])
        outs = []
        for g in range(SSM_GROUPS):
            yg = y[:, g * gw:(g + 1) * gw]
            ms = jnp.mean(yg * yg, axis=-1, keepdims=True)
            outs.append(yg * lax.rsqrt(ms + EPS) * nw_ref[:, g * gw:(g + 1) * gw])
        y_ref[0, r0:r0 + cl, :] = jnp.concatenate(outs, axis=1).astype(BF16)


def _ssd(xbc, z, dt, conv_w, conv_b, dt_bias, a_log, d_skip_full, norm_w, *, chunks=4):
    b, s, _ = xbc.shape
    rows = chunks * SSM_CHUNK
    blk = lambda width: pl.BlockSpec((1, rows, width), lambda bi, ci: (bi, ci, 0))
    const = lambda shape: pl.BlockSpec(shape, lambda bi, ci: (0, 0))
    return pl.pallas_call(
        functools.partial(_ssd_kernel, chunks=chunks),
        grid=(b, s // rows),
        in_specs=[blk(CONV_DIM), blk(D_SSM), blk(DT_PAD),
                  const((SSM_CONV, CONV_DIM)), const((1, CONV_DIM)), const((1, DT_PAD)),
                  const((1, DT_PAD)), const((1, D_SSM)), const((1, D_SSM))],
        out_specs=blk(D_SSM),
        out_shape=jax.ShapeDtypeStruct((b, s, D_SSM), BF16),
        scratch_shapes=[pltpu.VMEM((rows + 8, CONV_DIM), F32),
                        pltpu.VMEM((8, CONV_DIM), F32),
                        pltpu.VMEM((SSM_STATE, D_SSM), F32)],
        compiler_params=pltpu.CompilerParams(
            dimension_semantics=("parallel", "arbitrary"), vmem_limit_bytes=VMEM_LIMIT),
        name="ssd",
    )(xbc, z, dt, conv_w, conv_b, dt_bias, a_log, d_skip_full, norm_w)


MOBA_QT = LANES


def _moba_kernel(q_ref, k_ref, v_ref, o_ref, kmean_ref, vt_ref, rel_ref, relown_ref, q4t_ref,
                 bias_ref, so_ref, sa_ref, sb_ref, mo_ref, ma_ref, mb_ref, acc_ref, m_ref, l_ref,
                 *, n_blk):
    nh = N_ATTN_HEADS
    bs = MOBA_BLOCK
    cols = nh * bs
    qb = pl.program_id(1)

    col = lax.broadcasted_iota(jnp.int32, (1, cols), 1)
    col_head = col // bs
    slope = jnp.full((1, cols), SLOPES_C[-1], F32)
    for h in range(nh - 2, -1, -1):
        slope = jnp.where(col_head == h, SLOPES_C[h], slope)

    @pl.when(qb == 0)
    def _():
        for n in range(n_blk):
            kblk = k_ref[0, n * bs:(n + 1) * bs, :].astype(F32)
            kmean_ref[n:n + 1, :] = jnp.sum(kblk, axis=0, keepdims=True) * (1.0 / bs)
            vt_ref[n] = _vt_with_ones(v_ref[0, n * bs:(n + 1) * bs, :].astype(F32).T)
        c = lax.broadcasted_iota(jnp.int32, (bs, cols), 0)
        a = lax.broadcasted_iota(jnp.int32, (bs, cols), 1) % bs
        rel = -(slope * LOG2E) * (a - c).astype(F32)
        rel_ref[...] = rel
        relown_ref[...] = jnp.where(a >= c, rel, NEG_INF)

    qt = q_ref[0].astype(F32).T
    frow = lax.broadcasted_iota(jnp.int32, (ATTN_WIDTH, bs), 0) // HEAD_DIM
    for h in range(nh):
        q4t_ref[:, h * bs:(h + 1) * bs] = jnp.where(frow == h, qt, 0.0).astype(BF16)

    def block_gate():
        km = kmean_ref[...]
        km_hi = km.astype(BF16)
        km_lo = (km - km_hi.astype(F32)).astype(BF16)
        q4t = q4t_ref[...]
        return _dot(km_hi, q4t) + _dot(km_lo, q4t)

    def select_blocks(gate):
        blk =lax.broadcasted_iota(jnp.int32, (n_blk, cols), 0)
        blk_f = blk.astype(F32)
        past = blk < qb
        g = jnp.where(past, gate, NEG_INF)
        sel = jnp.zeros((n_blk, cols), jnp.bool_)
        for _ in range(MOBA_TOPK):
            mx = jnp.max(g, axis=0, keepdims=True)
            first = jnp.min(jnp.where(g == mx, blk_f, float(n_blk)), axis=0, keepdims=True)
            pick = blk_f == first
            sel = sel | pick
            g = jnp.where(pick, -jnp.inf, g)
        sel = sel & past
        bias_ref[...] = jnp.where(sel, -(slope * (LOG2E * bs)) * (qb - blk).astype(F32), NEG_INF)

    def scores(kb, buf, slot, rel_mat_ref, heads=range(N_ATTN_HEADS)):
        dst_ref, mx_ref = buf
        kb = jnp.minimum(kb, n_blk - 1)
        kblk = k_ref[0, pl.ds(pl.multiple_of(kb * bs, bs), bs), :]
        for h in heads:
            hs = slice(h * bs, (h + 1) * bs)
            s = _dot(kblk, q4t_ref[:, hs]) + rel_mat_ref[:, hs]
            dst_ref[slot, :, hs] = s
            for i in range(bs // MOBA_QT):
                t = h * bs // MOBA_QT + i
                mx_ref[slot, t:t + 1, :] = jnp.max(s[:, i * MOBA_QT:(i + 1) * MOBA_QT], axis=0, keepdims=True)

    def attend(kb, buf, bias_rows, slot=0, heads=range(N_ATTN_HEADS)):
        src_ref, mx_ref = buf
        first = bias_rows is None
        n = 1 if first else len(bias_rows)
        for h in heads:
            hs = slice(h * bs, (h + 1) * bs)
            ps, alphas = [[] for _ in range(n)], []
            tiles = range(h * bs // MOBA_QT, (h + 1) * bs // MOBA_QT)
            for t in tiles:
                cs = slice(t * MOBA_QT, (t + 1) * MOBA_QT)
                mxs = [mx_ref[slot + i, t:t + 1, :] for i in range(n)]
                if first:
                    m_new = mxs[0]
                    shifts = [m_new]
                else:
                    bs_t = [b[:, cs] for b in bias_rows]
                    m_old = m_ref[t:t + 1, :]
                    m_new = m_old
                    for mx, b in zip(mxs, bs_t):
                        m_new = jnp.maximum(m_new, mx + b)
                    shifts = [m_new - b for b in bs_t]
                    alphas.append(jnp.exp2(m_old - m_new))
                m_ref[t:t + 1, :] = m_new
                for i in range(n):
                    ps[i].append(jnp.exp2(src_ref[slot + i, :, cs] - shifts[i]).astype(BF16))
            p_all = jnp.concatenate([jnp.concatenate(p, axis=1) for p in ps], axis=0)
            vt = jnp.concatenate([vt_ref[kb + i, h * VT_ROWS:(h + 1) * VT_ROWS, :] for i in range(n)], axis=1)
            pv = _dot(vt, p_all)
            for i, t in enumerate(tiles):
                lsum = pv[HEAD_DIM:HEAD_DIM + 1, i * MOBA_QT:(i + 1) * MOBA_QT]
                l_ref[t:t + 1, :] = lsum if first else alphas[i] * l_ref[t:t + 1, :] + lsum
            pv = pv[0:HEAD_DIM, :]
            acc_ref[:, hs] = pv if first else jnp.concatenate(alphas, axis=1) * acc_ref[:, hs] + pv

    buf_own, buf_a, buf_b = (so_ref, mo_ref), (sa_ref, ma_ref), (sb_ref, mb_ref)
    gate = block_gate()
    scores(qb, buf_own, 0, relown_ref)
    scores(0, buf_a, 0, rel_ref)
    scores(1, buf_a, 1, rel_ref)
    attend(qb, buf_own, None)
    select_blocks(gate)

    def body(j, carry):
        kb = 2 * j

        def step(cur, nxt):
            for blk in range(2):
                bias_row = bias_ref[pl.ds(kb + blk, 1), :]
                for h in range(nh):
                    scores(kb + 2 + blk, nxt, blk, rel_ref, heads=(h,))
                    attend(kb + blk, cur, [bias_row], slot=blk, heads=(h,))

        @pl.when(j % 2 == 0)
        def _():
            step(buf_a, buf_b)

        @pl.when(j % 2 == 1)
        def _():
            step(buf_b, buf_a)

        return carry

    n_pairs = qb // 2
    lax.fori_loop(0, n_pairs, body, 0)

    for parity, cur in ((0, buf_a), (1, buf_b)):
        @pl.when((qb % 2 == 1) & (n_pairs % 2 == parity))
        def _(cur=cur):
            attend(qb - 1, cur, [bias_ref[pl.ds(qb - 1, 1), :]])

    inv = 1.0 / l_ref[...]
    heads = []
    for h in range(nh):
        tiles = range(h * bs // MOBA_QT, (h + 1) * bs // MOBA_QT)
        heads.append(jnp.concatenate(
            [acc_ref[:, t * MOBA_QT:(t + 1) * MOBA_QT] * inv[t:t + 1, :] for t in tiles], axis=1))
    o_ref[0] = jnp.concatenate(heads, axis=0).T.astype(BF16)


def _moba(qkv):
    b, s, _ = qkv.shape
    w = ATTN_WIDTH
    bs = MOBA_BLOCK
    n_blk = s // bs
    cols = N_ATTN_HEADS * bs
    return pl.pallas_call(
        functools.partial(_moba_kernel, n_blk=n_blk),
        grid=(b, n_blk),
        in_specs=[pl.BlockSpec((1, bs, w), lambda bi, qi: (bi, qi, 0)),
                  pl.BlockSpec((1, s, w), lambda bi, qi: (bi, 0, 1)),
                  pl.BlockSpec((1, s, w), lambda bi, qi: (bi, 0, 2))],
        out_specs=pl.BlockSpec((1, bs, w), lambda bi, qi: (bi, qi, 0)),
        out_shape=jax.ShapeDtypeStruct((b, s, w), BF16),
        scratch_shapes=[pltpu.VMEM((n_blk, w), F32),
                        pltpu.VMEM((n_blk, N_ATTN_HEADS * VT_ROWS, bs), BF16),
                        pltpu.VMEM((bs, cols), F32),
                        pltpu.VMEM((bs, cols), F32),
                        pltpu.VMEM((w, cols), BF16),
                        pltpu.VMEM((n_blk, cols), F32),
                        pltpu.VMEM((1, bs, cols), F32),
                        pltpu.VMEM((2, bs, cols), F32),
                        pltpu.VMEM((2, bs, cols), F32),
                        pltpu.VMEM((1, cols // MOBA_QT, MOBA_QT), F32),
                        pltpu.VMEM((2, cols // MOBA_QT, MOBA_QT), F32),
                        pltpu.VMEM((2, cols // MOBA_QT, MOBA_QT), F32),
                        pltpu.VMEM((HEAD_DIM, cols), F32),
                        pltpu.VMEM((cols // MOBA_QT, MOBA_QT), F32),
                        pltpu.VMEM((cols // MOBA_QT, MOBA_QT), F32)],
        compiler_params=pltpu.CompilerParams(
            dimension_semantics=("parallel", "arbitrary"), vmem_limit_bytes=VMEM_LIMIT),
        name="moba",
    )(qkv, qkv, qkv)


def _out_mlp_kernel(x_ref, ya_ref, yb_ref, yc_ref, wo_ref, n2_ref, w1_ref, w2_ref,
                    out_ref, h_ref, *, tf):
    tm = x_ref.shape[0]
    pr = tm // ROW_PARTS
    for s in range(ROW_PARTS):
        rows = slice(s * pr, (s + 1) * pr)
        y = jnp.concatenate([ya_ref[rows, :], yb_ref[rows, :], yc_ref[rows, :]], axis=1)
        x1 = x_ref[rows, :] + _dot(y, wo_ref[...])
        out_ref[rows, :] = x1
        ms = jnp.mean(x1 * x1, axis=-1, keepdims=True)
        h_ref[rows, :] = (x1 * lax.rsqrt(ms + EPS) * n2_ref[...]).astype(BF16)
    for c in range(D_FF // tf):
        cols = slice(c * tf, (c + 1) * tf)
        u = jnp.maximum(_dot(h_ref[...], w1_ref[:, cols]), 0.0)
        out_ref[...] += _dot((u * u).astype(BF16), w2_ref[cols, :])


def _out_mlp(x2d, ya, yb, yc, w_out, n2, w1, w2, *, tm=1024, tf=1024):
    t = x2d.shape[0]
    row = lambda width: pl.BlockSpec((tm, width), lambda i: (i, 0))
    resident = lambda shape: pl.BlockSpec(shape, lambda i: (0, 0), pipeline_mode=pl.Buffered(1))
    return pl.pallas_call(
        functools.partial(_out_mlp_kernel, tf=tf),
        grid=(t // tm,),
        in_specs=[row(D_MODEL), row(ATTN_WIDTH), row(D_SSM), row(ATTN_WIDTH),
                  resident((D_MODEL, D_MODEL)), resident((1, D_MODEL)),
                  resident((D_MODEL, D_FF)), resident((D_FF, D_MODEL))],
        out_specs=row(D_MODEL),
        out_shape=jax.ShapeDtypeStruct((t, D_MODEL), F32),
        scratch_shapes=[pltpu.VMEM((tm, D_MODEL), BF16)],
        compiler_params=pltpu.CompilerParams(
            dimension_semantics=("parallel",), vmem_limit_bytes=VMEM_LIMIT),
        name="out_mlp",
    )(x2d, ya, yb, yc, w_out, n2, w1, w2)


def _layer(x2d, bsz, seq, norm1_w, w_in, a_q_norm, a_k_norm, c_q_norm, c_k_norm, conv_w, conv_b,
           dt_bias, a_log, d_skip, ssm_norm_w, w_out, norm2_w, w_mlp_in, w_mlp_out):
    w_pad = jnp.pad(w_in, ((0, 0), (0, DT_PAD - SSM_HEADS))).astype(BF16)
    scale = HEAD_DIM ** -0.5 * LOG2E
    gains = jnp.stack([jnp.tile(a_q_norm, N_ATTN_HEADS) * scale, jnp.tile(a_k_norm, N_ATTN_HEADS),
                       jnp.tile(c_q_norm, N_ATTN_HEADS) * scale, jnp.tile(c_k_norm, N_ATTN_HEADS)])
    lane = jnp.arange(ATTN_WIDTH) // HEAD_DIM
    bd = (lane[:, None] == lane[None, :]).astype(BF16)
    pad_h = lambda v: jnp.pad(v, (0, DT_PAD - SSM_HEADS)).reshape(1, DT_PAD)

    qkva, qkvc, z, xbc, dt = _in_proj(x2d, norm1_w.reshape(1, -1), w_pad, gains, bd)

    ya = _dilated(qkva.reshape(bsz, seq, -1))
    yb =_ssd(xbc.reshape(bsz, seq, -1), z.reshape(bsz, seq, -1), dt.reshape(bsz, seq, -1),
              conv_w, conv_b.reshape(1, -1), pad_h(dt_bias), pad_h(a_log),
              jnp.repeat(d_skip, HEAD_DIM).reshape(1, -1), ssm_norm_w.reshape(1, -1))
    yc = _moba(qkvc.reshape(bsz, seq, -1))

    return _out_mlp(x2d, ya.reshape(bsz * seq, -1), yb.reshape(bsz * seq, -1), yc.reshape(bsz * seq, -1),
                    w_out.astype(BF16), norm2_w.reshape(1, -1),
                    w_mlp_in.astype(BF16), w_mlp_out.astype(BF16))


def kernel(x, norm1_w, w_in, a_q_norm, a_k_norm, c_q_norm, c_k_norm, conv_w, conv_b, dt_bias,
           a_log, d_skip, ssm_norm_w, w_out, norm2_w, w_mlp_in, w_mlp_out):
    bsz, seq, d = x.shape
    x2d = x.reshape(bsz * seq, d)
    for i in range(norm1_w.shape[0]):
        x2d = _layer(x2d, bsz, seq, norm1_w[i], w_in[i], a_q_norm[i], a_k_norm[i], c_q_norm[i],
                     c_k_norm[i], conv_w[i], conv_b[i], dt_bias[i], a_log[i], d_skip[i],
                     ssm_norm_w[i], w_out[i], norm2_w[i], w_mlp_in[i], w_mlp_out[i])
    return x2d.reshape(bsz, seq, d)
```

```python
import functools

import jax
import jax.numpy as jnp
from jax import lax
from jax.experimental import pallas as pl
from jax.experimental.pallas import tpu as pltpu

F32 = jnp.float32
BF16 = jnp.bfloat16

D_MODEL = 1024
HEAD_DIM = 64
N_ATTN_HEADS = 4
ATTN_WIDTH = N_ATTN_HEADS * HEAD_DIM
DILATED_PATTERNS = ((128, 1), (512, 4), (2048, 16))
BAND = 128
MOBA_BLOCK = 256
MOBA_TOPK = 3
SSM_HEADS = 8
D_SSM = SSM_HEADS * HEAD_DIM
SSM_STATE = 128
SSM_GROUPS = 2
SSM_CONV = 4
SSM_CHUNK = 128
CONV_DIM = D_SSM + 2 * SSM_GROUPS * SSM_STATE
D_FF = 4 * D_MODEL
D_IN = 2 * 3 * ATTN_WIDTH + D_SSM + CONV_DIM + SSM_HEADS
LANES = 128
DT_PAD = LANES
D_IN_PAD = D_IN - SSM_HEADS + DT_PAD
EPS = 1e-6
NEG_INF = -1e30
LOG2E = 1.4426950408889634
SLOPES_A = tuple(2.0 ** -(2 * h + 1) for h in range(N_ATTN_HEADS))
SLOPES_C = tuple(2.0 ** -(2 * h + 2) for h in range(N_ATTN_HEADS))

VMEM_LIMIT = 56 * 1024 * 1024
ROW_PARTS = 4


def _split3(a):
    hi = a.astype(BF16)
    r1 = a - hi.astype(F32)
    mid = r1.astype(BF16)
    lo = (r1 - mid.astype(F32)).astype(BF16)
    return hi, mid, lo


def _dot(a, b):
    return jnp.dot(a, b, preferred_element_type=F32)


def _dot_nt(a, b):
    return lax.dot_general(a, b, (((1,), (1,)), ((), ())), preferred_element_type=F32)


def _dot_tn(a, b):
    return lax.dot_general(a, b, (((0,), (0,)), ((), ())), preferred_element_type=F32)


def _dot_f32_lhs(a, sel):
    hi, mid, lo = _split3(a)
    return _dot(hi, sel) + _dot(mid, sel) + _dot(lo, sel)


def _dot_f32_rhs(sel, b):
    hi, mid, lo = _split3(b)
    return _dot(sel, hi) + _dot(sel, mid) + _dot(sel, lo)


VT_ROWS = HEAD_DIM + 16


def _vt_with_ones(vt):
    keys = vt.shape[1]
    ones_tile = (lax.broadcasted_iota(jnp.int32, (VT_ROWS - HEAD_DIM, keys), 0) == 0).astype(F32)
    parts = []
    for h in range(vt.shape[0] // HEAD_DIM):
        parts += [vt[h * HEAD_DIM:(h + 1) * HEAD_DIM, :], ones_tile]
    return jnp.concatenate(parts, axis=0).astype(BF16)


def _head_masks(width, dtype):
    lane = lax.broadcasted_iota(jnp.int32, (1, width), 1)
    return [(lane // HEAD_DIM == h).astype(dtype) for h in range(width // HEAD_DIM)]


def _row_slopes(rows_per_head, slopes):
    row = lax.broadcasted_iota(jnp.int32, (len(slopes) * rows_per_head, 1), 0) // rows_per_head
    out = jnp.full(row.shape, slopes[-1], F32)
    for h in range(len(slopes) - 2, -1, -1):
        out = jnp.where(row == h, slopes[h], out)
    return out


def _in_proj_kernel(x_ref, n1_ref, w_ref, g_ref, bd_ref,
                    qkva_ref, qkvc_ref, z_ref, xbc_ref, dt_ref, h_ref):
    pr = x_ref.shape[0] // ROW_PARTS
    for s in range(ROW_PARTS):
        rows = slice(s * pr, (s + 1) * pr)
        x = x_ref[rows, :]
        ms = jnp.mean(x * x, axis=-1, keepdims=True)
        h_ref[rows, :] = (x * lax.rsqrt(ms + EPS) * n1_ref[...]).astype(BF16)
    bd = bd_ref[...]

    def seg(lo, width):
        return _dot(h_ref[...], w_ref[:, lo:lo + width])

    def head_norm(y, g):
        ss = _dot((y * y).astype(BF16), bd)
        return y * lax.rsqrt(ss * (1.0 / HEAD_DIM) + EPS) * g

    w = ATTN_WIDTH
    for i, out_ref in enumerate((qkva_ref, qkvc_ref)):
        base = 3 * w * i
        out_ref[:, 0:w] = head_norm(seg(base, w), g_ref[2 * i:2 * i + 1, :]).astype(BF16)
        out_ref[:, w:2 * w] = head_norm(seg(base + w, w), g_ref[2 * i + 1:2 * i + 2, :]).astype(BF16)
        out_ref[:, 2 * w:3 * w] = seg(base + 2 * w, w).astype(BF16)
    z_ref[...] = seg(6 * w, D_SSM)
    xbc_ref[...] = seg(6 * w + D_SSM, CONV_DIM)
    dt_ref[...] = seg(6 * w + D_SSM + CONV_DIM, DT_PAD)


def _in_proj(x2d, n1, w_pad, gains, bd, *, tm=1024):
    t = x2d.shape[0]
    row = lambda i: (i, 0)
    resident = lambda shape: pl.BlockSpec(shape, lambda i: (0, 0), pipeline_mode=pl.Buffered(1))
    return pl.pallas_call(
        _in_proj_kernel,
        grid=(t // tm,),
        in_specs=[
            pl.BlockSpec((tm, D_MODEL), row),
            resident((1, D_MODEL)),
            resident((D_MODEL, D_IN_PAD)),
            resident((4, ATTN_WIDTH)),
            resident((ATTN_WIDTH, ATTN_WIDTH)),
        ],
        out_specs=[
            pl.BlockSpec((tm, 3 * ATTN_WIDTH), row),
            pl.BlockSpec((tm, 3 * ATTN_WIDTH), row),
            pl.BlockSpec((tm, D_SSM), row),
            pl.BlockSpec((tm, CONV_DIM), row),
            pl.BlockSpec((tm, DT_PAD), row),
        ],
        out_shape=[
            jax.ShapeDtypeStruct((t, 3 * ATTN_WIDTH), BF16),
            jax.ShapeDtypeStruct((t, 3 * ATTN_WIDTH), BF16),
            jax.ShapeDtypeStruct((t, D_SSM), F32),
            jax.ShapeDtypeStruct((t, CONV_DIM), F32),
            jax.ShapeDtypeStruct((t, DT_PAD), F32),
        ],
        scratch_shapes=[pltpu.VMEM((tm, D_MODEL), BF16)],
        compiler_params=pltpu.CompilerParams(
            dimension_semantics=("parallel",), vmem_limit_bytes=VMEM_LIMIT),
        name="in_proj",
    )(x2d, n1, w_pad, gains, bd)


N_SLABS = 3 * ATTN_WIDTH // LANES
COPY_ROWS = 512
GATHER_UNROLL = 4


def _dilated_kernel(qkv_ref, o_ref, slab_ref, qr_ref, kr_ref, vt_ref, bias_ref, sa_ref, sb_ref,
                    st_ref, y_ref, lse_ref):
    nh = N_ATTN_HEADS
    w = ATTN_WIDTH
    seq = qkv_ref.shape[1]
    n_tiles = seq // BAND
    hm_b = _head_masks(w, BF16)

    @pl.when(pl.program_id(0) == 0)
    def _():
        c = lax.broadcasted_iota(jnp.int32, (2 * BAND, nh * BAND), 0)
        col = lax.broadcasted_iota(jnp.int32, (2 * BAND, nh * BAND), 1)
        a = col % BAND
        head = col // BAND
        slope = jnp.full(col.shape, SLOPES_A[-1], F32)
        for h in range(nh - 2, -1, -1):
            slope = jnp.where(head == h, SLOPES_A[h], slope)
        for p, (_, dil) in enumerate(DILATED_PATTERNS):
            for later, offset in enumerate((a - c, a + BAND - c)):
                valid = (offset >= 0) & (offset <= BAND)
                bias_ref[p, later] = jnp.where(valid, -(slope * (LOG2E * dil)) * offset.astype(F32), NEG_INF)

    def copy(i, carry):
        rows = pl.ds(pl.multiple_of(i * COPY_ROWS, COPY_ROWS), COPY_ROWS)
        x = qkv_ref[0, rows, :].astype(F32)
        for j in range(N_SLABS):
            slab_ref[j, rows, :] = x[:, j * LANES:(j + 1) * LANES]
        return carry
    lax.fori_loop(0, seq // COPY_ROWS, copy, 0)

    def run(p, dil, first_pattern):
        nblk = seq // dil // BAND

        def sub_rows(t):
            if dil == 1:
                return pl.ds(pl.multiple_of(t * BAND, BAND), BAND)
            return pl.ds(t // nblk + dil * BAND * (t % nblk), BAND, stride=dil)

        def gather(i, carry):
            for u in range(GATHER_UNROLL):
                t = i * GATHER_UNROLL + u
                parts = [slab_ref[j, sub_rows(t), :] for j in range(N_SLABS)]
                dst = pl.ds(pl.multiple_of(t * BAND, BAND), BAND)
                qr_ref[dst, :] = jnp.concatenate(parts[0:2], axis=1).astype(BF16)
                kr_ref[dst, :] = jnp.concatenate(parts[2:4], axis=1).astype(BF16)
                vt_ref[t] = _vt_with_ones(jnp.concatenate(parts[4:6], axis=1).T)
            return carry
        lax.fori_loop(0, n_tiles // GATHER_UNROLL, gather, 0)

        def scores(t, dst_ref):
            tt = jnp.minimum(t, n_tiles - 1)
            later = jnp.minimum(tt % nblk, 1)
            q = qr_ref[pl.ds(pl.multiple_of(tt * BAND, BAND), BAND), :]
            q4 = jnp.concatenate([q * hm_b[h] for h in range(nh)], axis=0)
            k0 = pl.multiple_of((tt - later) * BAND, BAND)
            dst_ref[...] = _dot_nt(kr_ref[pl.ds(k0, 2 * BAND), :], q4) + bias_ref[p, later]

        n_slabs_out = w // LANES
        heads_per_slab = nh // n_slabs_out

        def softmax_pv(t, src_ref, slot, sl):
            kb = t - jnp.minimum(t % nblk, 1)
            for h in range(sl * heads_per_slab, (sl + 1) * heads_per_slab):
                cs = slice(h * BAND, (h + 1) * BAND)
                hr = slice(h * VT_ROWS, (h + 1) * VT_ROWS)
                fr = slice(h * HEAD_DIM, (h + 1) * HEAD_DIM)
                mx = jnp.max(src_ref[:, cs], axis=0, keepdims=True)
                pexp = jnp.exp2(src_ref[:, cs] - mx)
                vt = jnp.concatenate([vt_ref[kb, hr, :], vt_ref[kb + 1, hr, :]], axis=1)
                pv = _dot(vt, pexp.astype(BF16))
                l = pv[HEAD_DIM:HEAD_DIM + 1, :]
                st_ref[slot, 0, fr, :] = pv[0:HEAD_DIM, :] * (1.0 / l)
                st_ref[slot, 1, fr, :] = jnp.broadcast_to(mx + jnp.log2(l), (HEAD_DIM, BAND))

        def merge(m, slot, sl):
            fr = slice(sl * LANES, (sl + 1) * LANES)
            out = st_ref[slot, 0, fr, :].T
            lse = st_ref[slot, 1, fr, :].T
            rows = sub_rows(jnp.maximum(m, 0))
            if first_pattern:
                y_ref[sl, rows, :] = out
                lse_ref[sl, rows, :] = lse
            else:
                l_old = lse_ref[sl, rows, :]
                mx2 = jnp.maximum(l_old, lse)
                e_old = jnp.exp2(l_old - mx2)
                e_new = jnp.exp2(lse - mx2)
                den = e_old + e_new
                y_ref[sl, rows, :] = (y_ref[sl, rows, :] * e_old + out * e_new) / den
                lse_ref[sl, rows, :] = mx2 + jnp.log2(den)

        scores(0, sa_ref)
        st_ref[1, 0] = jnp.zeros(st_ref.shape[2:], F32)
        st_ref[1, 1] = jnp.full(st_ref.shape[2:], NEG_INF, F32)

        def pair(i, carry):
            t = 2 * i
            scores(t + 1, sb_ref)
            for sl in range(n_slabs_out):
                softmax_pv(t, sa_ref, 0, sl)
                merge(t - 1, 1, sl)
            scores(t + 2, sa_ref)
            for sl in range(n_slabs_out):
                softmax_pv(t + 1, sb_ref, 1, sl)
                merge(t, 0, sl)
            return carry
        lax.fori_loop(0, n_tiles // 2, pair, 0)
        for sl in range(n_slabs_out):
            merge(n_tiles - 1, 1, sl)

    order = sorted(range(len(DILATED_PATTERNS)), key=lambda p: -DILATED_PATTERNS[p][1])
    for p in order:
        run(p, DILATED_PATTERNS[p][1], p == order[0])

    def emit(i, carry):
        rows = pl.ds(pl.multiple_of(i * COPY_ROWS, COPY_ROWS), COPY_ROWS)
        o_ref[0, rows, :] = jnp.concatenate(
            [y_ref[sl, rows, :] for sl in range(w // LANES)], axis=1).astype(BF16)
        return carry
    lax.fori_loop(0, seq // COPY_ROWS, emit, 0)


def _dilated(qkv):
    b, s, _ = qkv.shape
    w = ATTN_WIDTH
    cols = N_ATTN_HEADS * BAND
    return pl.pallas_call(
        _dilated_kernel,
        grid=(b,),
        in_specs=[pl.BlockSpec((1, s, 3 * w), lambda bi: (bi, 0, 0))],
        out_specs=pl.BlockSpec((1, s, w), lambda bi: (bi, 0, 0)),
        out_shape=jax.ShapeDtypeStruct((b, s, w), BF16),
        scratch_shapes=[pltpu.VMEM((N_SLABS, s, LANES), F32),
                        pltpu.VMEM((s, w), BF16),
                        pltpu.VMEM((s, w), BF16),
                        pltpu.VMEM((s // BAND, N_ATTN_HEADS * VT_ROWS, BAND), BF16),
                        pltpu.VMEM((len(DILATED_PATTERNS), 2, 2 * BAND, cols), F32),
                        pltpu.VMEM((2 * BAND, cols), F32),
                        pltpu.VMEM((2 * BAND, cols), F32),
                        pltpu.VMEM((2, 2, w, BAND), F32),
                        pltpu.VMEM((w // LANES, s, LANES), F32),
                        pltpu.VMEM((w // LANES, s, LANES), F32)],
        compiler_params=pltpu.CompilerParams(
            dimension_semantics=("arbitrary",), vmem_limit_bytes=VMEM_LIMIT),
        name="dilated",
    )(qkv)


def _softplus(x):
    return jnp.maximum(x, 0.0) + jnp.log1p(jnp.exp(-jnp.abs(x)))


def _silu(x):
    return x / (1.0 + jnp.exp(-x))


def _ssd_kernel(xbc_ref, z_ref, dt_ref, cw_ref, cb_ref, dtb_ref, alog_ref, dsk_ref, nw_ref,
                y_ref, ext_ref, tail_ref, state_ref, *, chunks):
    rows = chunks * SSM_CHUNK
    halo = 8
    cl = SSM_CHUNK
    gw = D_SSM // SSM_GROUPS
    hpg = SSM_HEADS // SSM_GROUPS

    @pl.when(pl.program_id(1) == 0)
    def _():
        tail_ref[...] = jnp.zeros_like(tail_ref)
        state_ref[...] = jnp.zeros_like(state_ref)

    ext_ref[0:halo, :] = tail_ref[...]
    ext_ref[halo:halo + rows, :] = xbc_ref[0]
    tail_ref[...] = xbc_ref[0, rows - halo:rows, :]
    conv = cb_ref[...] + cw_ref[SSM_CONV - 1:SSM_CONV, :] * ext_ref[halo:halo + rows, :]
    for j in range(1, SSM_CONV):
        conv = conv + cw_ref[SSM_CONV - 1 - j:SSM_CONV - j, :] * ext_ref[halo - j:halo - j + rows, :]
    xact = _silu(conv)

    dt = _softplus(dt_ref[0] + dtb_ref[...])
    da = dt * (-jnp.exp(alog_ref[...]))

    ri = lax.broadcasted_iota(jnp.int32, (cl, cl), 0)
    ci = lax.broadcasted_iota(jnp.int32, (cl, cl), 1)
    tri = ri >= ci
    tril = tri.astype(BF16)
    eh = lax.broadcasted_iota(jnp.int32, (DT_PAD, D_SSM), 0)
    el = lax.broadcasted_iota(jnp.int32, (DT_PAD, D_SSM), 1) // HEAD_DIM
    expand = (eh == el).astype(BF16)
    hm_g = _head_masks(gw, BF16)

    for c in range(chunks):
        r0 = c * cl
        xs = xact[r0:r0 + cl, 0:D_SSM]
        bm = xact[r0:r0 + cl, D_SSM:D_SSM + SSM_GROUPS * SSM_STATE]
        cm = xact[r0:r0 + cl, D_SSM + SSM_GROUPS * SSM_STATE:]
        acs = _dot_f32_rhs(tril, da[r0:r0 + cl])
        acs_t = acs.T
        acs_full = _dot_f32_lhs(acs, expand)
        dt_full = _dot_f32_lhs(dt[r0:r0 + cl], expand)
        xd = xs * dt_full
        last = acs_full[cl - 1:cl, :]
        xdd = xd * jnp.exp(last - acs_full)
        exp_acs = jnp.exp(acs_full)
        chunk_decay = jnp.exp(last)
        y_parts = []
        for g in range(SSM_GROUPS):
            gs = slice(g * gw, (g + 1) * gw)
            bg = bm[:, g * SSM_STATE:(g + 1) * SSM_STATE].astype(BF16)
            cg = cm[:, g * SSM_STATE:(g + 1) * SSM_STATE].astype(BF16)
            cb = _dot_nt(cg, bg)
            s_in = state_ref[:, gs]
            y_g = _dot(cg, s_in.astype(BF16)) * exp_acs[:, gs]
            state_ref[:, gs] = s_in * chunk_decay[:, gs] + _dot_tn(bg, xdd[:, gs].astype(BF16))
            xd_g = xd[:, gs].astype(BF16)
            for hh in range(hpg):
                h = g * hpg + hh
                diff = acs[:, h:h + 1] - acs_t[h:h + 1, :]
                lmat = jnp.exp(jnp.where(tri, diff, NEG_INF))
                y_g = y_g + _dot((cb * lmat).astype(BF16), xd_g * hm_g[hh])
            y_parts.append(y_g)
        y = jnp.concatenate(y_parts, axis=1) + dsk_ref[...] * xs
        y = y * _silu(z_ref[0, r0:r0 + cl, :])
        outs = []
        for g in range(SSM_GROUPS):
            yg = y[:, g * gw:(g + 1) * gw]
            ms = jnp.mean(yg * yg, axis=-1, keepdims=True)
            outs.append(yg * lax.rsqrt(ms + EPS) * nw_ref[:, g * gw:(g + 1) * gw])
        y_ref[0, r0:r0 + cl, :] = jnp.concatenate(outs, axis=1).astype(BF16)


def _ssd(xbc, z, dt, conv_w, conv_b, dt_bias, a_log, d_skip_full, norm_w, *, chunks=4):
    b, s, _ = xbc.shape
    rows = chunks * SSM_CHUNK
    blk = lambda width: pl.BlockSpec((1, rows, width), lambda bi, ci: (bi, ci, 0))
    const = lambda shape: pl.BlockSpec(shape, lambda bi, ci: (0, 0))
    return pl.pallas_call(
        functools.partial(_ssd_kernel, chunks=chunks),
        grid=(b, s // rows),
        in_specs=[blk(CONV_DIM), blk(D_SSM), blk(DT_PAD),
                  const((SSM_CONV, CONV_DIM)), const((1, CONV_DIM)), const((1, DT_PAD)),
                  const((1, DT_PAD)), const((1, D_SSM)), const((1, D_SSM))],
        out_specs=blk(D_SSM),
        out_shape=jax.ShapeDtypeStruct((b, s, D_SSM), BF16),
        scratch_shapes=[pltpu.VMEM((rows + 8, CONV_DIM), F32),
                        pltpu.VMEM((8, CONV_DIM), F32),
                        pltpu.VMEM((SSM_STATE, D_SSM), F32)],
        compiler_params=pltpu.CompilerParams(
            dimension_semantics=("parallel", "arbitrary"), vmem_limit_bytes=VMEM_LIMIT),
        name="ssd",
    )(xbc, z, dt, conv_w, conv_b, dt_bias, a_log, d_skip_full, norm_w)


MOBA_QT = LANES


def _moba_kernel(q_ref, k_ref, v_ref, o_ref, kmean_ref, vt_ref, rel_ref, relown_ref, q4t_ref,
                 bias_ref, so_ref, sa_ref, sb_ref, mo_ref, ma_ref, mb_ref, acc_ref, m_ref, l_ref,
                 *, n_blk):
    nh = N_ATTN_HEADS
    bs = MOBA_BLOCK
    cols = nh * bs
    qb = pl.program_id(1)

    col = lax.broadcasted_iota(jnp.int32, (1, cols), 1)
    col_head = col // bs
    slope = jnp.full((1, cols), SLOPES_C[-1], F32)
    for h in range(nh - 2, -1, -1):
        slope = jnp.where(col_head == h, SLOPES_C[h], slope)

    @pl.when(qb == 0)
    def _():
        for n in range(n_blk):
            kblk = k_ref[0, n * bs:(n + 1) * bs, :].astype(F32)
            kmean_ref[n:n + 1, :] = jnp.sum(kblk, axis=0, keepdims=True) * (1.0 / bs)
            vt_ref[n] = _vt_with_ones(v_ref[0, n * bs:(n + 1) * bs, :].astype(F32).T)
        c = lax.broadcasted_iota(jnp.int32, (bs, cols), 0)
        a = lax.broadcasted_iota(jnp.int32, (bs, cols), 1) % bs
        rel = -(slope * LOG2E) * (a - c).astype(F32)
        rel_ref[...] = rel
        relown_ref[...] = jnp.where(a >= c, rel, NEG_INF)

    qt = q_ref[0].astype(F32).T
    frow = lax.broadcasted_iota(jnp.int32, (ATTN_WIDTH, bs), 0) // HEAD_DIM
    for h in range(nh):
        q4t_ref[:, h * bs:(h + 1) * bs] = jnp.where(frow == h, qt, 0.0).astype(BF16)

    def gate_rows():
        km = kmean_ref[...]
        km_hi = km.astype(BF16)
        km_lo = (km - km_hi.astype(F32)).astype(BF16)
        return jnp.concatenate([km_hi, km_lo], axis=0)

    def select_blocks(gate):
        blk =lax.broadcasted_iota(jnp.int32, (n_blk, cols), 0)
        blk_f = blk.astype(F32)
        past = blk < qb
        g = jnp.where(past, gate, NEG_INF)
        sel = jnp.zeros((n_blk, cols), jnp.bool_)
        for _ in range(MOBA_TOPK):
            mx = jnp.max(g, axis=0, keepdims=True)
            first = jnp.min(jnp.where(g == mx, blk_f, float(n_blk)), axis=0, keepdims=True)
            pick = blk_f == first
            sel = sel | pick
            g = jnp.where(pick, -jnp.inf, g)
        sel = sel & past
        bias_ref[...] = jnp.where(sel, -(slope * (LOG2E * bs)) * (qb - blk).astype(F32), NEG_INF)

    def scores(kb, buf, slot, rel_mat_ref, heads=range(N_ATTN_HEADS), extra_rows=None):
        dst_ref, mx_ref = buf
        kb = jnp.minimum(kb, n_blk - 1)
        lhs = k_ref[0, pl.ds(pl.multiple_of(kb * bs, bs), bs), :]
        if extra_rows is not None:
            lhs = jnp.concatenate([lhs, extra_rows], axis=0)
        extras = []
        for h in heads:
            hs = slice(h * bs, (h + 1) * bs)
            prod = _dot(lhs, q4t_ref[:, hs])
            s = prod[0:bs, :] + rel_mat_ref[:, hs]
            dst_ref[slot, :, hs] = s
            for i in range(bs // MOBA_QT):
                t = h * bs // MOBA_QT + i
                mx_ref[slot, t:t + 1, :] = jnp.max(s[:, i * MOBA_QT:(i + 1) * MOBA_QT], axis=0, keepdims=True)
            if extra_rows is not None:
                extras.append(prod[bs:, :])
        return jnp.concatenate(extras, axis=1) if extras else None

    def attend(kb, buf, bias_rows, slot=0, heads=range(N_ATTN_HEADS)):
        src_ref, mx_ref = buf
        first = bias_rows is None
        n = 1 if first else len(bias_rows)
        for h in heads:
            hs = slice(h * bs, (h + 1) * bs)
            ps, alphas = [[] for _ in range(n)], []
            tiles = range(h * bs // MOBA_QT, (h + 1) * bs // MOBA_QT)
            for t in tiles:
                cs = slice(t * MOBA_QT, (t + 1) * MOBA_QT)
                mxs = [mx_ref[slot + i, t:t + 1, :] for i in range(n)]
                if first:
                    m_new = mxs[0]
                    shifts = [m_new]
                else:
                    bs_t = [b[:, cs] for b in bias_rows]
                    m_old = m_ref[t:t + 1, :]
                    m_new = m_old
                    for mx, b in zip(mxs, bs_t):
                        m_new = jnp.maximum(m_new, mx + b)
                    shifts = [m_new - b for b in bs_t]
                    alphas.append(jnp.exp2(m_old - m_new))
                m_ref[t:t + 1, :] = m_new
                for i in range(n):
                    ps[i].append(jnp.exp2(src_ref[slot + i, :, cs] - shifts[i]).astype(BF16))
            p_all = jnp.concatenate([jnp.concatenate(p, axis=1) for p in ps], axis=0)
            vt = jnp.concatenate([vt_ref[kb + i, h * VT_ROWS:(h + 1) * VT_ROWS, :] for i in range(n)], axis=1)
            pv = _dot(vt, p_all)
            for i, t in enumerate(tiles):
                lsum = pv[HEAD_DIM:HEAD_DIM + 1, i * MOBA_QT:(i + 1) * MOBA_QT]
                l_ref[t:t + 1, :] = lsum if first else alphas[i] * l_ref[t:t + 1, :] + lsum
            pv = pv[0:HEAD_DIM, :]
            acc_ref[:, hs] = pv if first else jnp.concatenate(alphas, axis=1) * acc_ref[:, hs] + pv

    buf_own, buf_a, buf_b = (so_ref, mo_ref), (sa_ref, ma_ref), (sb_ref, mb_ref)
    gate2 = scores(qb, buf_own, 0, relown_ref, extra_rows=gate_rows())
    gate = gate2[0:n_blk, :] + gate2[n_blk:, :]
    scores(0, buf_a, 0, rel_ref)
    scores(1, buf_a, 1, rel_ref)
    attend(qb, buf_own, None)
    select_blocks(gate)

    def body(j, carry):
        kb = 2 * j

        def step(cur, nxt):
            for blk in range(2):
                bias_row = bias_ref[pl.ds(kb + blk, 1), :]
                for h in range(nh):
                    scores(kb + 2 + blk, nxt, blk, rel_ref, heads=(h,))
                    attend(kb + blk, cur, [bias_row], slot=blk, heads=(h,))

        @pl.when(j % 2 == 0)
        def _():
            step(buf_a, buf_b)

        @pl.when(j % 2 == 1)
        def _():
            step(buf_b, buf_a)

        return carry

    n_pairs = qb // 2
    lax.fori_loop(0, n_pairs, body, 0)

    for parity, cur in ((0, buf_a), (1, buf_b)):
        @pl.when((qb % 2 == 1) & (n_pairs % 2 == parity))
        def _(cur=cur):
            attend(qb - 1, cur, [bias_ref[pl.ds(qb - 1, 1), :]])

    inv = 1.0 / l_ref[...]
    heads = []
    for h in range(nh):
        tiles = range(h * bs // MOBA_QT, (h + 1) * bs // MOBA_QT)
        heads.append(jnp.concatenate(
            [acc_ref[:, t * MOBA_QT:(t + 1) * MOBA_QT] * inv[t:t + 1, :] for t in tiles], axis=1))
    o_ref[0] = jnp.concatenate(heads, axis=0).T.astype(BF16)


def _moba(qkv):
    b, s, _ = qkv.shape
    w = ATTN_WIDTH
    bs = MOBA_BLOCK
    n_blk = s // bs
    cols = N_ATTN_HEADS * bs
    return pl.pallas_call(
        functools.partial(_moba_kernel, n_blk=n_blk),
        grid=(b, n_blk),
        in_specs=[pl.BlockSpec((1, bs, w), lambda bi, qi: (bi, qi, 0)),
                  pl.BlockSpec((1, s, w), lambda bi, qi: (bi, 0, 1)),
                  pl.BlockSpec((1, s, w), lambda bi, qi: (bi, 0, 2))],
        out_specs=pl.BlockSpec((1, bs, w), lambda bi, qi: (bi, qi, 0)),
        out_shape=jax.ShapeDtypeStruct((b, s, w), BF16),
        scratch_shapes=[pltpu.VMEM((n_blk, w), F32),
                        pltpu.VMEM((n_blk, N_ATTN_HEADS * VT_ROWS, bs), BF16),
                        pltpu.VMEM((bs, cols), F32),
                        pltpu.VMEM((bs, cols), F32),
                        pltpu.VMEM((w, cols), BF16),
                        pltpu.VMEM((n_blk, cols), F32),
                        pltpu.VMEM((1, bs, cols), F32),
                        pltpu.VMEM((2, bs, cols), F32),
                        pltpu.VMEM((2, bs, cols), F32),
                        pltpu.VMEM((1, cols // MOBA_QT, MOBA_QT), F32),
                        pltpu.VMEM((2, cols // MOBA_QT, MOBA_QT), F32),
                        pltpu.VMEM((2, cols // MOBA_QT, MOBA_QT), F32),
                        pltpu.VMEM((HEAD_DIM, cols), F32),
                        pltpu.VMEM((cols // MOBA_QT, MOBA_QT), F32),
                        pltpu.VMEM((cols // MOBA_QT, MOBA_QT), F32)],
        compiler_params=pltpu.CompilerParams(
            dimension_semantics=("parallel", "arbitrary"), vmem_limit_bytes=VMEM_LIMIT),
        name="moba",
    )(qkv, qkv, qkv)


def _out_mlp_kernel(x_ref, ya_ref, yb_ref, yc_ref, wo_ref, n2_ref, w1_ref, w2_ref,
                    out_ref, h_ref, *, tf):
    tm = x_ref.shape[0]
    pr = tm // ROW_PARTS
    for s in range(ROW_PARTS):
        rows = slice(s * pr, (s + 1) * pr)
        y = jnp.concatenate([ya_ref[rows, :], yb_ref[rows, :], yc_ref[rows, :]], axis=1)
        x1 = x_ref[rows, :] + _dot(y, wo_ref[...])
        out_ref[rows, :] = x1
        ms = jnp.mean(x1 * x1, axis=-1, keepdims=True)
        h_ref[rows, :] = (x1 * lax.rsqrt(ms + EPS) * n2_ref[...]).astype(BF16)
    for c in range(D_FF // tf):
        cols = slice(c * tf, (c + 1) * tf)
        u = jnp.maximum(_dot(h_ref[...], w1_ref[:, cols]), 0.0)
        out_ref[...] += _dot((u * u).astype(BF16), w2_ref[cols, :])


def _out_mlp(x2d, ya, yb, yc, w_out, n2, w1, w2, *, tm=1024, tf=1024):
    t = x2d.shape[0]
    row = lambda width: pl.BlockSpec((tm, width), lambda i: (i, 0))
    resident = lambda shape: pl.BlockSpec(shape, lambda i: (0, 0), pipeline_mode=pl.Buffered(1))
    return pl.pallas_call(
        functools.partial(_out_mlp_kernel, tf=tf),
        grid=(t // tm,),
        in_specs=[row(D_MODEL), row(ATTN_WIDTH), row(D_SSM), row(ATTN_WIDTH),
                  resident((D_MODEL, D_MODEL)), resident((1, D_MODEL)),
                  resident((D_MODEL, D_FF)), resident((D_FF, D_MODEL))],
        out_specs=row(D_MODEL),
        out_shape=jax.ShapeDtypeStruct((t, D_MODEL), F32),
        scratch_shapes=[pltpu.VMEM((tm, D_MODEL), BF16)],
        compiler_params=pltpu.CompilerParams(
            dimension_semantics=("parallel",), vmem_limit_bytes=VMEM_LIMIT),
        name="out_mlp",
    )(x2d, ya, yb, yc, w_out, n2, w1, w2)


def _layer(x2d, bsz, seq, norm1_w, w_in, a_q_norm, a_k_norm, c_q_norm, c_k_norm, conv_w, conv_b,
           dt_bias, a_log, d_skip, ssm_norm_w, w_out, norm2_w, w_mlp_in, w_mlp_out):
    w_pad = jnp.pad(w_in, ((0, 0), (0, DT_PAD - SSM_HEADS))).astype(BF16)
    scale = HEAD_DIM ** -0.5 * LOG2E
    gains = jnp.stack([jnp.tile(a_q_norm, N_ATTN_HEADS) * scale, jnp.tile(a_k_norm, N_ATTN_HEADS),
                       jnp.tile(c_q_norm, N_ATTN_HEADS) * scale, jnp.tile(c_k_norm, N_ATTN_HEADS)])
    lane = jnp.arange(ATTN_WIDTH) // HEAD_DIM
    bd = (lane[:, None] == lane[None, :]).astype(BF16)
    pad_h = lambda v: jnp.pad(v, (0, DT_PAD - SSM_HEADS)).reshape(1, DT_PAD)

    qkva, qkvc, z, xbc, dt = _in_proj(x2d, norm1_w.reshape(1, -1), w_pad, gains, bd)

    ya = _dilated(qkva.reshape(bsz, seq, -1))
    yb =_ssd(xbc.reshape(bsz, seq, -1), z.reshape(bsz, seq, -1), dt.reshape(bsz, seq, -1),
              conv_w, conv_b.reshape(1, -1), pad_h(dt_bias), pad_h(a_log),
              jnp.repeat(d_skip, HEAD_DIM).reshape(1, -1), ssm_norm_w.reshape(1, -1))
    yc = _moba(qkvc.reshape(bsz, seq, -1))

    return _out_mlp(x2d, ya.reshape(bsz * seq, -1), yb.reshape(bsz * seq, -1), yc.reshape(bsz * seq, -1),
                    w_out.astype(BF16), norm2_w.reshape(1, -1),
                    w_mlp_in.astype(BF16), w_mlp_out.astype(BF16))


def kernel(x, norm1_w, w_in, a_q_norm, a_k_norm, c_q_norm, c_k_norm, conv_w, conv_b, dt_bias,
           a_log, d_skip, ssm_norm_w, w_out, norm2_w, w_mlp_in, w_mlp_out):
    bsz, seq, d = x.shape
    x2d = x.reshape(bsz * seq, d)
    for i in range(norm1_w.shape[0]):
        x2d = _layer(x2d, bsz, seq, norm1_w[i], w_in[i], a_q_norm[i], a_k_norm[i], c_q_norm[i],
                     c_k_norm[i], conv_w[i], conv_b[i], dt_bias[i], a_log[i], d_skip[i],
                     ssm_norm_w[i], w_out[i], norm2_w[i], w_mlp_in[i], w_mlp_out[i])
    return x2d.reshape(bsz, seq, d)
```

```python
import functools

import jax
import jax.numpy as jnp
from jax import lax
from jax.experimental import pallas as pl
from jax.experimental.pallas import tpu as pltpu

F32 = jnp.float32
BF16 = jnp.bfloat16

D_MODEL = 1024
HEAD_DIM = 64
N_ATTN_HEADS = 4
ATTN_WIDTH = N_ATTN_HEADS * HEAD_DIM
DILATED_PATTERNS = ((128, 1), (512, 4), (2048, 16))
BAND = 128
MOBA_BLOCK = 256
MOBA_TOPK = 3
SSM_HEADS = 8
D_SSM = SSM_HEADS * HEAD_DIM
SSM_STATE = 128
SSM_GROUPS = 2
SSM_CONV = 4
SSM_CHUNK = 128
CONV_DIM = D_SSM + 2 * SSM_GROUPS * SSM_STATE
D_FF = 4 * D_MODEL
D_IN = 2 * 3 * ATTN_WIDTH + D_SSM + CONV_DIM + SSM_HEADS
LANES = 128
SUBLANES_F32 = 8
SUBLANES_BF16 = 16
DT_PAD = LANES
D_IN_PAD = D_IN - SSM_HEADS + DT_PAD
EPS = 1e-6
NEG_INF = -1e30
LOG2E = 1.4426950408889634
SLOPES_A = tuple(2.0 ** -(2 * h + 1) for h in range(N_ATTN_HEADS))
SLOPES_C = tuple(2.0 ** -(2 * h + 2) for h in range(N_ATTN_HEADS))

VMEM_LIMIT = 56 * 1024 * 1024
ROW_PARTS = 4


def _split3(a):
    hi = a.astype(BF16)
    r1 = a - hi.astype(F32)
    mid = r1.astype(BF16)
    lo = (r1 - mid.astype(F32)).astype(BF16)
    return hi, mid, lo


def _dot(a, b):
    return jnp.dot(a, b, preferred_element_type=F32)


def _dot_nt(a, b):
    return lax.dot_general(a, b, (((1,), (1,)), ((), ())), preferred_element_type=F32)


def _dot_tn(a, b):
    return lax.dot_general(a, b, (((0,), (0,)), ((), ())), preferred_element_type=F32)


def _dot_f32_lhs(a, sel):
    hi, mid, lo = _split3(a)
    return _dot(hi, sel) + _dot(mid, sel) + _dot(lo, sel)


def _dot_f32_rhs(sel, b):
    hi, mid, lo = _split3(b)
    return _dot(sel, hi) + _dot(sel, mid) + _dot(sel, lo)


VT_ROWS = HEAD_DIM + SUBLANES_BF16


def _vt_with_ones(vt):
    keys = vt.shape[1]
    ones_tile = (lax.broadcasted_iota(jnp.int32, (VT_ROWS - HEAD_DIM, keys), 0) == 0).astype(F32)
    parts = []
    for h in range(vt.shape[0] // HEAD_DIM):
        parts += [vt[h * HEAD_DIM:(h + 1) * HEAD_DIM, :], ones_tile]
    return jnp.concatenate(parts, axis=0).astype(BF16)


def _head_masks(width, dtype):
    lane = lax.broadcasted_iota(jnp.int32, (1, width), 1)
    return [(lane // HEAD_DIM == h).astype(dtype) for h in range(width // HEAD_DIM)]


def _row_slopes(rows_per_head, slopes):
    row = lax.broadcasted_iota(jnp.int32, (len(slopes) * rows_per_head, 1), 0) // rows_per_head
    out = jnp.full(row.shape, slopes[-1], F32)
    for h in range(len(slopes) - 2, -1, -1):
        out = jnp.where(row == h, slopes[h], out)
    return out


def _in_proj_kernel(x_ref, n1_ref, w_ref, g_ref, bd_ref,
                    qkva_ref, qkvc_ref, z_ref, xbc_ref, dt_ref, h_ref):
    pr = x_ref.shape[0] // ROW_PARTS
    for s in range(ROW_PARTS):
        rows = slice(s * pr, (s + 1) * pr)
        x = x_ref[rows, :]
        ms = jnp.mean(x * x, axis=-1, keepdims=True)
        h_ref[rows, :] = (x * lax.rsqrt(ms + EPS) * n1_ref[...]).astype(BF16)
    bd = bd_ref[...]

    def seg(lo, width):
        return _dot(h_ref[...], w_ref[:, lo:lo + width])

    def head_norm(y, g):
        ss = _dot((y * y).astype(BF16), bd)
        return y * lax.rsqrt(ss * (1.0 / HEAD_DIM) + EPS) * g

    w = ATTN_WIDTH
    for i, out_ref in enumerate((qkva_ref, qkvc_ref)):
        base = 3 * w * i
        out_ref[:, 0:w] = head_norm(seg(base, w), g_ref[2 * i:2 * i + 1, :]).astype(BF16)
        out_ref[:, w:2 * w] = head_norm(seg(base + w, w), g_ref[2 * i + 1:2 * i + 2, :]).astype(BF16)
        out_ref[:, 2 * w:3 * w] = seg(base + 2 * w, w).astype(BF16)
    z_ref[...] = seg(6 * w, D_SSM)
    xbc_ref[...] = seg(6 * w + D_SSM, CONV_DIM)
    dt_ref[...] = seg(6 * w + D_SSM + CONV_DIM, DT_PAD)


def _in_proj(x2d, n1, w_pad, gains, bd, *, tm=1024):
    t = x2d.shape[0]
    row = lambda i: (i, 0)
    resident = lambda shape: pl.BlockSpec(shape, lambda i: (0, 0), pipeline_mode=pl.Buffered(1))
    return pl.pallas_call(
        _in_proj_kernel,
        grid=(t // tm,),
        in_specs=[
            pl.BlockSpec((tm, D_MODEL), row),
            resident((1, D_MODEL)),
            resident((D_MODEL, D_IN_PAD)),
            resident((4, ATTN_WIDTH)),
            resident((ATTN_WIDTH, ATTN_WIDTH)),
        ],
        out_specs=[
            pl.BlockSpec((tm, 3 * ATTN_WIDTH), row),
            pl.BlockSpec((tm, 3 * ATTN_WIDTH), row),
            pl.BlockSpec((tm, D_SSM), row),
            pl.BlockSpec((tm, CONV_DIM), row),
            pl.BlockSpec((tm, DT_PAD), row),
        ],
        out_shape=[
            jax.ShapeDtypeStruct((t, 3 * ATTN_WIDTH), BF16),
            jax.ShapeDtypeStruct((t, 3 * ATTN_WIDTH), BF16),
            jax.ShapeDtypeStruct((t, D_SSM), F32),
            jax.ShapeDtypeStruct((t, CONV_DIM), F32),
            jax.ShapeDtypeStruct((t, DT_PAD), F32),
        ],
        scratch_shapes=[pltpu.VMEM((tm, D_MODEL), BF16)],
        compiler_params=pltpu.CompilerParams(
            dimension_semantics=("parallel",), vmem_limit_bytes=VMEM_LIMIT),
        name="in_proj",
    )(x2d, n1, w_pad, gains, bd)


N_SLABS = 3 * ATTN_WIDTH // LANES
COPY_ROWS = 512
GATHER_UNROLL = 4
TWO_LEVEL_STRIDE = 4


def _dilated_kernel(qkv_ref, o_ref, slab_ref, tmp_ref, qr_ref, kr_ref, vt_ref, bias_ref, sa_ref, sb_ref,
                    st_ref, y_ref, lse_ref):
    nh = N_ATTN_HEADS
    w = ATTN_WIDTH
    seq = qkv_ref.shape[1]
    n_tiles = seq // BAND
    hm_b = _head_masks(w, BF16)

    @pl.when(pl.program_id(0) == 0)
    def _():
        c = lax.broadcasted_iota(jnp.int32, (2 * BAND, nh * BAND), 0)
        col = lax.broadcasted_iota(jnp.int32, (2 * BAND, nh * BAND), 1)
        a = col % BAND
        head = col // BAND
        slope = jnp.full(col.shape, SLOPES_A[-1], F32)
        for h in range(nh - 2, -1, -1):
            slope = jnp.where(head == h, SLOPES_A[h], slope)
        for p, (_, dil) in enumerate(DILATED_PATTERNS):
            for later, offset in enumerate((a - c, a + BAND - c)):
                valid = (offset >= 0) & (offset <= BAND)
                bias_ref[p, later] = jnp.where(valid, -(slope * (LOG2E * dil)) * offset.astype(F32), NEG_INF)

    def copy(i, carry):
        rows = pl.ds(pl.multiple_of(i * COPY_ROWS, COPY_ROWS), COPY_ROWS)
        x = qkv_ref[0, rows, :].astype(F32)
        for j in range(N_SLABS):
            slab_ref[j, rows, :] = x[:, j * LANES:(j + 1) * LANES]
        return carry
    lax.fori_loop(0, seq // COPY_ROWS, copy, 0)

    def run(p, dil, first_pattern):
        nblk = seq // dil // BAND

        def sub_rows(t):
            if dil == 1:
                return pl.ds(pl.multiple_of(t * BAND, BAND), BAND)
            return pl.ds(t // nblk + dil * BAND * (t % nblk), BAND, stride=dil)

        def gather(i, carry):
            for u in range(GATHER_UNROLL):
                t = i * GATHER_UNROLL + u
                parts = [slab_ref[j, sub_rows(t), :] for j in range(N_SLABS)]
                dst = pl.ds(pl.multiple_of(t * BAND, BAND), BAND)
                qr_ref[dst, :] = jnp.concatenate(parts[0:2], axis=1).astype(BF16)
                kr_ref[dst, :] = jnp.concatenate(parts[2:4], axis=1).astype(BF16)
                vt_ref[t] = _vt_with_ones(jnp.concatenate(parts[4:6], axis=1).T)
            return carry

        def gather_two_level(j):
            d1 = TWO_LEVEL_STRIDE
            d2 = dil // d1
            sec = seq // d1
            bps = n_tiles // d1
            lanes = slice((j % 2) * LANES, (j % 2 + 1) * LANES)
            ones_rows = (lax.broadcasted_iota(jnp.int32, (VT_ROWS - HEAD_DIM, BAND), 0) == 0).astype(BF16)

            def first(bb, carry):
                for r1 in range(d1):
                    src = pl.ds(r1 + d1 * BAND * bb, BAND, stride=d1)
                    dst = pl.ds(pl.multiple_of(r1 * sec + bb * BAND, BAND), BAND)
                    tmp_ref[dst, :] = slab_ref[j, src, :]
                return carry
            lax.fori_loop(0, bps, first, 0)

            def second(q, carry):
                for r1 in range(d1):
                    for m in range(nblk):
                        t = (d1 * q + r1) * nblk + m
                        src = pl.ds(r1 * sec + d2 * BAND * m + q, BAND, stride=d2)
                        part = tmp_ref[src, :]
                        dst = pl.ds(pl.multiple_of(t * BAND, BAND), BAND)
                        if j < 2:
                            qr_ref[dst, lanes] = part.astype(BF16)
                        elif j < 4:
                            kr_ref[dst, lanes] = part.astype(BF16)
                        else:
                            part_t = part.T
                            for hh in range(LANES // HEAD_DIM):
                                h0 = ((j % 2) * (LANES // HEAD_DIM) + hh) * VT_ROWS
                                vt_ref[t, h0:h0 + HEAD_DIM, :] = (
                                    part_t[hh * HEAD_DIM:(hh + 1) * HEAD_DIM, :].astype(BF16))
                                vt_ref[t, h0 + HEAD_DIM:h0 + VT_ROWS, :] = ones_rows
                return carry
            lax.fori_loop(0, d2, second, 0)

        if dil % (TWO_LEVEL_STRIDE * TWO_LEVEL_STRIDE) == 0:
            for j in range(N_SLABS):
                gather_two_level(j)
        else:
            lax.fori_loop(0, n_tiles // GATHER_UNROLL, gather, 0)

        def scores(t, dst_ref):
            tt = jnp.minimum(t, n_tiles - 1)
            later = jnp.minimum(tt % nblk, 1)
            q = qr_ref[pl.ds(pl.multiple_of(tt * BAND, BAND), BAND), :]
            q4 = jnp.concatenate([q * hm_b[h] for h in range(nh)], axis=0)
            k0 = pl.multiple_of((tt - later) * BAND, BAND)
            dst_ref[...] = _dot_nt(kr_ref[pl.ds(k0, 2 * BAND), :], q4) + bias_ref[p, later]

        n_slabs_out = w // LANES
        heads_per_slab = nh // n_slabs_out

        def softmax_pv(t, src_ref, slot, sl):
            kb = t - jnp.minimum(t % nblk, 1)
            for h in range(sl * heads_per_slab, (sl + 1) * heads_per_slab):
                cs = slice(h * BAND, (h + 1) * BAND)
                hr = slice(h * VT_ROWS, (h + 1) * VT_ROWS)
                fr = slice(h * HEAD_DIM, (h + 1) * HEAD_DIM)
                mx = jnp.max(src_ref[:, cs], axis=0, keepdims=True)
                pexp = jnp.exp2(src_ref[:, cs] - mx)
                vt = jnp.concatenate([vt_ref[kb, hr, :], vt_ref[kb + 1, hr, :]], axis=1)
                pv = _dot(vt, pexp.astype(BF16))
                l = pv[HEAD_DIM:HEAD_DIM + 1, :]
                st_ref[slot, 0, fr, :] = pv[0:HEAD_DIM, :] * (1.0 / l)
                st_ref[slot, 1, fr, :] = jnp.broadcast_to(mx + jnp.log2(l), (HEAD_DIM, BAND))

        def merge(m, slot, sl):
            fr = slice(sl * LANES, (sl + 1) * LANES)
            out = st_ref[slot, 0, fr, :].T
            lse = st_ref[slot, 1, fr, :].T
            rows = sub_rows(jnp.maximum(m, 0))
            if first_pattern:
                y_ref[sl, rows, :] = out
                lse_ref[sl, rows, :] = lse
            else:
                l_old = lse_ref[sl, rows, :]
                mx2 = jnp.maximum(l_old, lse)
                e_old = jnp.exp2(l_old - mx2)
                e_new = jnp.exp2(lse - mx2)
                den = e_old + e_new
                y_ref[sl, rows, :] = (y_ref[sl, rows, :] * e_old + out * e_new) / den
                lse_ref[sl, rows, :] = mx2 + jnp.log2(den)

        scores(0, sa_ref)
        st_ref[1, 0] = jnp.zeros(st_ref.shape[2:], F32)
        st_ref[1, 1] = jnp.full(st_ref.shape[2:], NEG_INF, F32)

        def pair(i, carry):
            t = 2 * i
            scores(t + 1, sb_ref)
            for sl in range(n_slabs_out):
                softmax_pv(t, sa_ref, 0, sl)
                merge(t - 1, 1, sl)
            scores(t + 2, sa_ref)
            for sl in range(n_slabs_out):
                softmax_pv(t + 1, sb_ref, 1, sl)
                merge(t, 0, sl)
            return carry
        lax.fori_loop(0, n_tiles // 2, pair, 0)
        for sl in range(n_slabs_out):
            merge(n_tiles - 1, 1, sl)

    order = sorted(range(len(DILATED_PATTERNS)), key=lambda p: -DILATED_PATTERNS[p][1])
    for p in order:
        run(p, DILATED_PATTERNS[p][1], p == order[0])

    def emit(i, carry):
        rows = pl.ds(pl.multiple_of(i * COPY_ROWS, COPY_ROWS), COPY_ROWS)
        o_ref[0, rows, :] = jnp.concatenate(
            [y_ref[sl, rows, :] for sl in range(w // LANES)], axis=1).astype(BF16)
        return carry
    lax.fori_loop(0, seq // COPY_ROWS, emit, 0)


def _dilated(qkv):
    b, s, _ = qkv.shape
    w = ATTN_WIDTH
    cols = N_ATTN_HEADS * BAND
    return pl.pallas_call(
        _dilated_kernel,
        grid=(b,),
        in_specs=[pl.BlockSpec((1, s, 3 * w), lambda bi: (bi, 0, 0))],
        out_specs=pl.BlockSpec((1, s, w), lambda bi: (bi, 0, 0)),
        out_shape=jax.ShapeDtypeStruct((b, s, w), BF16),
        scratch_shapes=[pltpu.VMEM((N_SLABS, s, LANES), F32),
                        pltpu.VMEM((s, LANES), F32),
                        pltpu.VMEM((s, w), BF16),
                        pltpu.VMEM((s, w), BF16),
                        pltpu.VMEM((s // BAND, N_ATTN_HEADS * VT_ROWS, BAND), BF16),
                        pltpu.VMEM((len(DILATED_PATTERNS), 2, 2 * BAND, cols), F32),
                        pltpu.VMEM((2 * BAND, cols), F32),
                        pltpu.VMEM((2 * BAND, cols), F32),
                        pltpu.VMEM((2, 2, w, BAND), F32),
                        pltpu.VMEM((w // LANES, s, LANES), F32),
                        pltpu.VMEM((w // LANES, s, LANES), F32)],
        compiler_params=pltpu.CompilerParams(
            dimension_semantics=("arbitrary",), vmem_limit_bytes=VMEM_LIMIT),
        name="dilated",
    )(qkv)


CONV_HALO = SUBLANES_F32


def _softplus(x):
    return jnp.maximum(x, 0.0) + jnp.log1p(jnp.exp(-jnp.abs(x)))


def _silu(x):
    return x / (1.0 + jnp.exp(-x))


def _ssd_kernel(xbc_ref, z_ref, dt_ref, cw_ref, cb_ref, dtb_ref, alog_ref, dsk_ref, nw_ref,
                y_ref, ext_ref, tail_ref, state_ref, *, chunks):
    rows = chunks * SSM_CHUNK
    halo = CONV_HALO
    cl = SSM_CHUNK
    gw = D_SSM // SSM_GROUPS
    hpg = SSM_HEADS // SSM_GROUPS

    @pl.when(pl.program_id(1) == 0)
    def _():
        tail_ref[...] = jnp.zeros_like(tail_ref)
        state_ref[...] = jnp.zeros_like(state_ref)

    ext_ref[0:halo, :] = tail_ref[...]
    ext_ref[halo:halo + rows, :] = xbc_ref[0]
    tail_ref[...] = xbc_ref[0, rows - halo:rows, :]
    conv = cb_ref[...] + cw_ref[SSM_CONV - 1:SSM_CONV, :] * ext_ref[halo:halo + rows, :]
    for j in range(1, SSM_CONV):
        conv = conv + cw_ref[SSM_CONV - 1 - j:SSM_CONV - j, :] * ext_ref[halo - j:halo - j + rows, :]
    xact = _silu(conv)

    dt = _softplus(dt_ref[0] + dtb_ref[...])
    da = dt * (-jnp.exp(alog_ref[...]))

    ri = lax.broadcasted_iota(jnp.int32, (cl, cl), 0)
    ci = lax.broadcasted_iota(jnp.int32, (cl, cl), 1)
    tri = ri >= ci
    tril = tri.astype(BF16)
    eh = lax.broadcasted_iota(jnp.int32, (DT_PAD, D_SSM), 0)
    el = lax.broadcasted_iota(jnp.int32, (DT_PAD, D_SSM), 1) // HEAD_DIM
    expand = (eh == el).astype(BF16)
    hm_g = _head_masks(gw, BF16)

    for c in range(chunks):
        r0 = c * cl
        xs = xact[r0:r0 + cl, 0:D_SSM]
        bm = xact[r0:r0 + cl, D_SSM:D_SSM + SSM_GROUPS * SSM_STATE]
        cm = xact[r0:r0 + cl, D_SSM + SSM_GROUPS * SSM_STATE:]
        acs = _dot_f32_rhs(tril, da[r0:r0 + cl])
        acs_t = acs.T
        acs_full = _dot_f32_lhs(acs, expand)
        dt_full = _dot_f32_lhs(dt[r0:r0 + cl], expand)
        xd = xs * dt_full
        last = acs_full[cl - 1:cl, :]
        xdd = xd * jnp.exp(last - acs_full)
        exp_acs = jnp.exp(acs_full)
        chunk_decay = jnp.exp(last)
        y_parts = []
        for g in range(SSM_GROUPS):
            gs = slice(g * gw, (g + 1) * gw)
            bg = bm[:, g * SSM_STATE:(g + 1) * SSM_STATE].astype(BF16)
            cg = cm[:, g * SSM_STATE:(g + 1) * SSM_STATE].astype(BF16)
            cb = _dot_nt(cg, bg)
            s_in = state_ref[:, gs]
            y_g = _dot(cg, s_in.astype(BF16)) * exp_acs[:, gs]
            state_ref[:, gs] = s_in * chunk_decay[:, gs] + _dot_tn(bg, xdd[:, gs].astype(BF16))
            xd_g = xd[:, gs].astype(BF16)
            for hh in range(hpg):
                h = g * hpg + hh
                diff = acs[:, h:h + 1] - acs_t[h:h + 1, :]
                lmat = jnp.exp(jnp.where(tri, diff, NEG_INF))
                y_g = y_g + _dot((cb * lmat).astype(BF16), xd_g * hm_g[hh])
            y_parts.append(y_g)
        y = jnp.concatenate(y_parts, axis=1) + dsk_ref[...] * xs
        y = y * _silu(z_ref[0, r0:r0 + cl, :])
        outs = []
        for g in range(SSM_GROUPS):
            yg = y[:, g * gw:(g + 1) * gw]
            ms = jnp.mean(yg * yg, axis=-1, keepdims=True)
            outs.append(yg * lax.rsqrt(ms + EPS) * nw_ref[:, g * gw:(g + 1) * gw])
        y_ref[0, r0:r0 + cl, :] = jnp.concatenate(outs, axis=1).astype(BF16)


def _ssd(xbc, z, dt, conv_w, conv_b, dt_bias, a_log, d_skip_full, norm_w, *, chunks=4):
    b, s, _ = xbc.shape
    rows = chunks * SSM_CHUNK
    blk = lambda width: pl.BlockSpec((1, rows, width), lambda bi, ci: (bi, ci, 0))
    const = lambda shape: pl.BlockSpec(shape, lambda bi, ci: (0, 0))
    return pl.pallas_call(
        functools.partial(_ssd_kernel, chunks=chunks),
        grid=(b, s // rows),
        in_specs=[blk(CONV_DIM), blk(D_SSM), blk(DT_PAD),
                  const((SSM_CONV, CONV_DIM)), const((1, CONV_DIM)), const((1, DT_PAD)),
                  const((1, DT_PAD)), const((1, D_SSM)), const((1, D_SSM))],
        out_specs=blk(D_SSM),
        out_shape=jax.ShapeDtypeStruct((b, s, D_SSM), BF16),
        scratch_shapes=[pltpu.VMEM((rows + CONV_HALO, CONV_DIM), F32),
                        pltpu.VMEM((CONV_HALO, CONV_DIM), F32),
                        pltpu.VMEM((SSM_STATE, D_SSM), F32)],
        compiler_params=pltpu.CompilerParams(
            dimension_semantics=("parallel", "arbitrary"), vmem_limit_bytes=VMEM_LIMIT),
        name="ssd",
    )(xbc, z, dt, conv_w, conv_b, dt_bias, a_log, d_skip_full, norm_w)


MOBA_QT = LANES


def _moba_kernel(q_ref, k_ref, v_ref, o_ref, kmean_ref, vt_ref, rel_ref, relown_ref, q4t_ref,
                 bias_ref, so_ref, sa_ref, sb_ref, mo_ref, ma_ref, mb_ref, acc_ref, m_ref, l_ref,
                 *, n_blk):
    nh = N_ATTN_HEADS
    bs = MOBA_BLOCK
    cols = nh * bs
    qb = pl.program_id(1)

    col = lax.broadcasted_iota(jnp.int32, (1, cols), 1)
    col_head = col // bs
    slope = jnp.full((1, cols), SLOPES_C[-1], F32)
    for h in range(nh - 2, -1, -1):
        slope = jnp.where(col_head == h, SLOPES_C[h], slope)

    @pl.when(qb == 0)
    def _():
        for n in range(n_blk):
            kblk = k_ref[0, n * bs:(n + 1) * bs, :].astype(F32)
            kmean_ref[n:n + 1, :] = jnp.sum(kblk, axis=0, keepdims=True) * (1.0 / bs)
            vt_ref[n] = _vt_with_ones(v_ref[0, n * bs:(n + 1) * bs, :].astype(F32).T)
        c = lax.broadcasted_iota(jnp.int32, (bs, cols), 0)
        a = lax.broadcasted_iota(jnp.int32, (bs, cols), 1) % bs
        rel = -(slope * LOG2E) * (a - c).astype(F32)
        rel_ref[...] = rel
        relown_ref[...] = jnp.where(a >= c, rel, NEG_INF)

    qt = q_ref[0].astype(F32).T
    frow = lax.broadcasted_iota(jnp.int32, (ATTN_WIDTH, bs), 0) // HEAD_DIM
    for h in range(nh):
        q4t_ref[:, h * bs:(h + 1) * bs] = jnp.where(frow == h, qt, 0.0).astype(BF16)

    def gate_rows():
        km = kmean_ref[...]
        km_hi = km.astype(BF16)
        km_lo = (km - km_hi.astype(F32)).astype(BF16)
        return jnp.concatenate([km_hi, km_lo], axis=0)

    def select_blocks(gate):
        blk =lax.broadcasted_iota(jnp.int32, (n_blk, cols), 0)
        blk_f = blk.astype(F32)
        past = blk < qb
        g = jnp.where(past, gate, NEG_INF)
        sel = jnp.zeros((n_blk, cols), jnp.bool_)
        for _ in range(MOBA_TOPK):
            mx = jnp.max(g, axis=0, keepdims=True)
            first = jnp.min(jnp.where(g == mx, blk_f, float(n_blk)), axis=0, keepdims=True)
            pick = blk_f == first
            sel = sel | pick
            g = jnp.where(pick, -jnp.inf, g)
        sel = sel & past
        bias_ref[...] = jnp.where(sel, -(slope * (LOG2E * bs)) * (qb - blk).astype(F32), NEG_INF)

    def scores(kb, buf, slot, rel_mat_ref, heads=range(N_ATTN_HEADS), extra_rows=None):
        dst_ref, mx_ref = buf
        kb = jnp.minimum(kb, n_blk - 1)
        lhs = k_ref[0, pl.ds(pl.multiple_of(kb * bs, bs), bs), :]
        if extra_rows is not None:
            lhs = jnp.concatenate([lhs, extra_rows], axis=0)
        extras = []
        for h in heads:
            hs = slice(h * bs, (h + 1) * bs)
            prod = _dot(lhs, q4t_ref[:, hs])
            s = prod[0:bs, :] + rel_mat_ref[:, hs]
            dst_ref[slot, :, hs] = s
            for i in range(bs // MOBA_QT):
                t = h * bs // MOBA_QT + i
                mx_ref[slot, t:t + 1, :] = jnp.max(s[:, i * MOBA_QT:(i + 1) * MOBA_QT], axis=0, keepdims=True)
            if extra_rows is not None:
                extras.append(prod[bs:, :])
        return jnp.concatenate(extras, axis=1) if extras else None

    def attend(kb, buf, bias_rows, slot=0, heads=range(N_ATTN_HEADS)):
        src_ref, mx_ref = buf
        first = bias_rows is None
        n = 1 if first else len(bias_rows)
        for h in heads:
            hs = slice(h * bs, (h + 1) * bs)
            ps, alphas = [[] for _ in range(n)], []
            tiles = range(h * bs // MOBA_QT, (h + 1) * bs // MOBA_QT)
            for t in tiles:
                cs = slice(t * MOBA_QT, (t + 1) * MOBA_QT)
                mxs = [mx_ref[slot + i, t:t + 1, :] for i in range(n)]
                if first:
                    m_new = mxs[0]
                    shifts = [m_new]
                else:
                    bs_t = [b[:, cs] for b in bias_rows]
                    m_old = m_ref[t:t + 1, :]
                    m_new = m_old
                    for mx, b in zip(mxs, bs_t):
                        m_new = jnp.maximum(m_new, mx + b)
                    shifts = [m_new - b for b in bs_t]
                    alphas.append(jnp.exp2(m_old - m_new))
                m_ref[t:t + 1, :] = m_new
                for i in range(n):
                    ps[i].append(jnp.exp2(src_ref[slot + i, :, cs] - shifts[i]).astype(BF16))
            p_all = jnp.concatenate([jnp.concatenate(p, axis=1) for p in ps], axis=0)
            vt = jnp.concatenate([vt_ref[kb + i, h * VT_ROWS:(h + 1) * VT_ROWS, :] for i in range(n)], axis=1)
            pv = _dot(vt, p_all)
            for i, t in enumerate(tiles):
                lsum = pv[HEAD_DIM:HEAD_DIM + 1, i * MOBA_QT:(i + 1) * MOBA_QT]
                l_ref[t:t + 1, :] = lsum if first else alphas[i] * l_ref[t:t + 1, :] + lsum
            pv = pv[0:HEAD_DIM, :]
            acc_ref[:, hs] = pv if first else jnp.concatenate(alphas, axis=1) * acc_ref[:, hs] + pv

    buf_own, buf_a, buf_b = (so_ref, mo_ref), (sa_ref, ma_ref), (sb_ref, mb_ref)
    gate2 = scores(qb, buf_own, 0, relown_ref, extra_rows=gate_rows())
    gate = gate2[0:n_blk, :] + gate2[n_blk:, :]
    scores(0, buf_a, 0, rel_ref)
    scores(1, buf_a, 1, rel_ref)
    attend(qb, buf_own, None)
    select_blocks(gate)

    def body(j, carry):
        kb = 2 * j

        def step(cur, nxt):
            for blk in range(2):
                bias_row = bias_ref[pl.ds(kb + blk, 1), :]
                for h in range(nh):
                    scores(kb + 2 + blk, nxt, blk, rel_ref, heads=(h,))
                    attend(kb + blk, cur, [bias_row], slot=blk, heads=(h,))

        @pl.when(j % 2 == 0)
        def _():
            step(buf_a, buf_b)

        @pl.when(j % 2 == 1)
        def _():
            step(buf_b, buf_a)

        return carry

    n_pairs = qb // 2
    lax.fori_loop(0, n_pairs, body, 0)

    for parity, cur in ((0, buf_a), (1, buf_b)):
        @pl.when((qb % 2 == 1) & (n_pairs % 2 == parity))
        def _(cur=cur):
            attend(qb - 1, cur, [bias_ref[pl.ds(qb - 1, 1), :]])

    inv = 1.0 / l_ref[...]
    heads = []
    for h in range(nh):
        tiles = range(h * bs // MOBA_QT, (h + 1) * bs // MOBA_QT)
        heads.append(jnp.concatenate(
            [acc_ref[:, t * MOBA_QT:(t + 1) * MOBA_QT] * inv[t:t + 1, :] for t in tiles], axis=1))
    o_ref[0] = jnp.concatenate(heads, axis=0).T.astype(BF16)


def _moba(qkv):
    b, s, _ = qkv.shape
    w = ATTN_WIDTH
    bs = MOBA_BLOCK
    n_blk = s // bs
    cols = N_ATTN_HEADS * bs
    return pl.pallas_call(
        functools.partial(_moba_kernel, n_blk=n_blk),
        grid=(b, n_blk),
        in_specs=[pl.BlockSpec((1, bs, w), lambda bi, qi: (bi, qi, 0)),
                  pl.BlockSpec((1, s, w), lambda bi, qi: (bi, 0, 1)),
                  pl.BlockSpec((1, s, w), lambda bi, qi: (bi, 0, 2))],
        out_specs=pl.BlockSpec((1, bs, w), lambda bi, qi: (bi, qi, 0)),
        out_shape=jax.ShapeDtypeStruct((b, s, w), BF16),
        scratch_shapes=[pltpu.VMEM((n_blk, w), F32),
                        pltpu.VMEM((n_blk, N_ATTN_HEADS * VT_ROWS, bs), BF16),
                        pltpu.VMEM((bs, cols), F32),
                        pltpu.VMEM((bs, cols), F32),
                        pltpu.VMEM((w, cols), BF16),
                        pltpu.VMEM((n_blk, cols), F32),
                        pltpu.VMEM((1, bs, cols), F32),
                        pltpu.VMEM((2, bs, cols), F32),
                        pltpu.VMEM((2, bs, cols), F32),
                        pltpu.VMEM((1, cols // MOBA_QT, MOBA_QT), F32),
                        pltpu.VMEM((2, cols // MOBA_QT, MOBA_QT), F32),
                        pltpu.VMEM((2, cols // MOBA_QT, MOBA_QT), F32),
                        pltpu.VMEM((HEAD_DIM, cols), F32),
                        pltpu.VMEM((cols // MOBA_QT, MOBA_QT), F32),
                        pltpu.VMEM((cols // MOBA_QT, MOBA_QT), F32)],
        compiler_params=pltpu.CompilerParams(
            dimension_semantics=("parallel", "arbitrary"), vmem_limit_bytes=VMEM_LIMIT),
        name="moba",
    )(qkv, qkv, qkv)


def _out_mlp_kernel(x_ref, ya_ref, yb_ref, yc_ref, wo_ref, n2_ref, w1_ref, w2_ref,
                    out_ref, h_ref, *, tf):
    tm = x_ref.shape[0]
    pr = tm // ROW_PARTS
    for s in range(ROW_PARTS):
        rows = slice(s * pr, (s + 1) * pr)
        y = jnp.concatenate([ya_ref[rows, :], yb_ref[rows, :], yc_ref[rows, :]], axis=1)
        x1 = x_ref[rows, :] + _dot(y, wo_ref[...])
        out_ref[rows, :] = x1
        ms = jnp.mean(x1 * x1, axis=-1, keepdims=True)
        h_ref[rows, :] = (x1 * lax.rsqrt(ms + EPS) * n2_ref[...]).astype(BF16)
    for c in range(D_FF // tf):
        cols = slice(c * tf, (c + 1) * tf)
        u = jnp.maximum(_dot(h_ref[...], w1_ref[:, cols]), 0.0)
        out_ref[...] += _dot((u * u).astype(BF16), w2_ref[cols, :])


def _out_mlp(x2d, ya, yb, yc, w_out, n2, w1, w2, *, tm=1024, tf=1024):
    t = x2d.shape[0]
    row = lambda width: pl.BlockSpec((tm, width), lambda i: (i, 0))
    resident = lambda shape: pl.BlockSpec(shape, lambda i: (0, 0), pipeline_mode=pl.Buffered(1))
    return pl.pallas_call(
        functools.partial(_out_mlp_kernel, tf=tf),
        grid=(t // tm,),
        in_specs=[row(D_MODEL), row(ATTN_WIDTH), row(D_SSM), row(ATTN_WIDTH),
                  resident((D_MODEL, D_MODEL)), resident((1, D_MODEL)),
                  resident((D_MODEL, D_FF)), resident((D_FF, D_MODEL))],
        out_specs=row(D_MODEL),
        out_shape=jax.ShapeDtypeStruct((t, D_MODEL), F32),
        scratch_shapes=[pltpu.VMEM((tm, D_MODEL), BF16)],
        compiler_params=pltpu.CompilerParams(
            dimension_semantics=("parallel",), vmem_limit_bytes=VMEM_LIMIT),
        name="out_mlp",
    )(x2d, ya, yb, yc, w_out, n2, w1, w2)


def _layer(x2d, bsz, seq, norm1_w, w_in, a_q_norm, a_k_norm, c_q_norm, c_k_norm, conv_w, conv_b,
           dt_bias, a_log, d_skip, ssm_norm_w, w_out, norm2_w, w_mlp_in, w_mlp_out):
    w_pad = jnp.pad(w_in, ((0, 0), (0, DT_PAD - SSM_HEADS))).astype(BF16)
    scale = HEAD_DIM ** -0.5 * LOG2E
    gains = jnp.stack([jnp.tile(a_q_norm, N_ATTN_HEADS) * scale, jnp.tile(a_k_norm, N_ATTN_HEADS),
                       jnp.tile(c_q_norm, N_ATTN_HEADS) * scale, jnp.tile(c_k_norm, N_ATTN_HEADS)])
    lane = jnp.arange(ATTN_WIDTH) // HEAD_DIM
    bd = (lane[:, None] == lane[None, :]).astype(BF16)
    pad_h = lambda v: jnp.pad(v, (0, DT_PAD - SSM_HEADS)).reshape(1, DT_PAD)

    qkva, qkvc, z, xbc, dt = _in_proj(x2d, norm1_w.reshape(1, -1), w_pad, gains, bd)

    ya = _dilated(qkva.reshape(bsz, seq, -1))
    yb = _ssd(xbc.reshape(bsz, seq, -1), z.reshape(bsz, seq, -1), dt.reshape(bsz, seq, -1),
              conv_w, conv_b.reshape(1, -1), pad_h(dt_bias), pad_h(a_log),
              jnp.repeat(d_skip, HEAD_DIM).reshape(1, -1), ssm_norm_w.reshape(1, -1))
    yc = _moba(qkvc.reshape(bsz, seq, -1))

    return _out_mlp(x2d, ya.reshape(bsz * seq, -1), yb.reshape(bsz * seq, -1), yc.reshape(bsz * seq, -1),
                    w_out.astype(BF16), norm2_w.reshape(1, -1),
                    w_mlp_in.astype(BF16), w_mlp_out.astype(BF16))


def kernel(x, norm1_w, w_in, a_q_norm, a_k_norm, c_q_norm, c_k_norm, conv_w, conv_b, dt_bias,
           a_log, d_skip, ssm_norm_w, w_out, norm2_w, w_mlp_in, w_mlp_out):
    bsz, seq, d = x.shape
    x2d = x.reshape(bsz * seq, d)
    for i in range(norm1_w.shape[0]):
        x2d = _layer(x2d, bsz, seq, norm1_w[i], w_in[i], a_q_norm[i], a_k_norm[i], c_q_norm[i],
                     c_k_norm[i], conv_w[i], conv_b[i], dt_bias[i], a_log[i], d_skip[i],
                     ssm_norm_w[i], w_out[i], norm2_w[i], w_mlp_in[i], w_mlp_out[i])
    return x2d.reshape(bsz, seq, d)
```

```python
import functools

import jax
import jax.numpy as jnp
from jax import lax
from jax.experimental import pallas as pl
from jax.experimental.pallas import tpu as pltpu

F32 = jnp.float32
BF16 = jnp.bfloat16

D_MODEL = 1024
HEAD_DIM = 64
N_ATTN_HEADS = 4
ATTN_WIDTH = N_ATTN_HEADS * HEAD_DIM
DILATED_PATTERNS = ((128, 1), (512, 4), (2048, 16))
BAND = 128
MOBA_BLOCK = 256
MOBA_TOPK = 3
SSM_HEADS = 8
D_SSM = SSM_HEADS * HEAD_DIM
SSM_STATE = 128
SSM_GROUPS = 2
SSM_CONV = 4
SSM_CHUNK = 128
CONV_DIM = D_SSM + 2 * SSM_GROUPS * SSM_STATE
D_FF = 4 * D_MODEL
D_IN = 2 * 3 * ATTN_WIDTH + D_SSM + CONV_DIM + SSM_HEADS
LANES = 128
SUBLANES_F32 = 8
SUBLANES_BF16 = 16
DT_PAD = LANES
D_IN_PAD = D_IN - SSM_HEADS + DT_PAD
EPS = 1e-6
NEG_INF = -1e30
LOG2E = 1.4426950408889634
SLOPES_A = tuple(2.0 ** -(2 * h + 1) for h in range(N_ATTN_HEADS))
SLOPES_C = tuple(2.0 ** -(2 * h + 2) for h in range(N_ATTN_HEADS))

VMEM_LIMIT = 56 * 1024 * 1024
ROW_PARTS = 4


def _split3(a):
    hi = a.astype(BF16)
    r1 = a - hi.astype(F32)
    mid = r1.astype(BF16)
    lo = (r1 - mid.astype(F32)).astype(BF16)
    return hi, mid, lo


def _dot(a, b):
    return jnp.dot(a, b, preferred_element_type=F32)


def _dot_nt(a, b):
    return lax.dot_general(a, b, (((1,), (1,)), ((), ())), preferred_element_type=F32)


def _dot_tn(a, b):
    return lax.dot_general(a, b, (((0,), (0,)), ((), ())), preferred_element_type=F32)


def _dot_f32_lhs(a, sel):
    hi, mid, lo = _split3(a)
    return _dot(hi, sel) + _dot(mid, sel) + _dot(lo, sel)


def _dot_f32_rhs(sel, b):
    hi, mid, lo = _split3(b)
    return _dot(sel, hi) + _dot(sel, mid) + _dot(sel, lo)


VT_ROWS = HEAD_DIM + SUBLANES_BF16


def _vt_with_ones(vt):
    keys = vt.shape[1]
    ones_tile = (lax.broadcasted_iota(jnp.int32, (VT_ROWS - HEAD_DIM, keys), 0) == 0).astype(F32)
    parts = []
    for h in range(vt.shape[0] // HEAD_DIM):
        parts += [vt[h * HEAD_DIM:(h + 1) * HEAD_DIM, :], ones_tile]
    return jnp.concatenate(parts, axis=0).astype(BF16)


def _head_masks(width, dtype):
    lane = lax.broadcasted_iota(jnp.int32, (1, width), 1)
    return [(lane // HEAD_DIM == h).astype(dtype) for h in range(width // HEAD_DIM)]


def _row_slopes(rows_per_head, slopes):
    row = lax.broadcasted_iota(jnp.int32, (len(slopes) * rows_per_head, 1), 0) // rows_per_head
    out = jnp.full(row.shape, slopes[-1], F32)
    for h in range(len(slopes) - 2, -1, -1):
        out = jnp.where(row == h, slopes[h], out)
    return out


def _in_proj_kernel(x_ref, n1_ref, w_ref, g_ref, bd_ref,
                    qkva_ref, qkvc_ref, z_ref, xbc_ref, dt_ref, h_ref):
    pr = x_ref.shape[0] // ROW_PARTS
    for s in range(ROW_PARTS):
        rows = slice(s * pr, (s + 1) * pr)
        x = x_ref[rows, :]
        ms = jnp.mean(x * x, axis=-1, keepdims=True)
        h_ref[rows, :] = (x * lax.rsqrt(ms + EPS) * n1_ref[...]).astype(BF16)
    bd = bd_ref[...]

    def seg(lo, width):
        return _dot(h_ref[...], w_ref[:, lo:lo + width])

    def head_norm(y, g):
        ss = _dot((y * y).astype(BF16), bd)
        return y * lax.rsqrt(ss * (1.0 / HEAD_DIM) + EPS) * g

    w = ATTN_WIDTH
    for i, out_ref in enumerate((qkva_ref, qkvc_ref)):
        base = 3 * w * i
        out_ref[:, 0:w] = head_norm(seg(base, w), g_ref[2 * i:2 * i + 1, :]).astype(BF16)
        out_ref[:, w:2 * w] = head_norm(seg(base + w, w), g_ref[2 * i + 1:2 * i + 2, :]).astype(BF16)
        out_ref[:, 2 * w:3 * w] = seg(base + 2 * w, w).astype(BF16)
    z_ref[...] = seg(6 * w, D_SSM)
    xbc_ref[...] = seg(6 * w + D_SSM, CONV_DIM)
    dt_ref[...] = seg(6 * w + D_SSM + CONV_DIM, DT_PAD)


def _in_proj(x2d, n1, w_pad, gains, bd, *, tm=1024):
    t = x2d.shape[0]
    row = lambda i: (i, 0)
    resident = lambda shape: pl.BlockSpec(shape, lambda i: (0, 0), pipeline_mode=pl.Buffered(1))
    return pl.pallas_call(
        _in_proj_kernel,
        grid=(t // tm,),
        in_specs=[
            pl.BlockSpec((tm, D_MODEL), row),
            resident((1, D_MODEL)),
            resident((D_MODEL, D_IN_PAD)),
            resident((4, ATTN_WIDTH)),
            resident((ATTN_WIDTH, ATTN_WIDTH)),
        ],
        out_specs=[
            pl.BlockSpec((tm, 3 * ATTN_WIDTH), row),
            pl.BlockSpec((tm, 3 * ATTN_WIDTH), row),
            pl.BlockSpec((tm, D_SSM), row),
            pl.BlockSpec((tm, CONV_DIM), row),
            pl.BlockSpec((tm, DT_PAD), row),
        ],
        out_shape=[
            jax.ShapeDtypeStruct((t, 3 * ATTN_WIDTH), BF16),
            jax.ShapeDtypeStruct((t, 3 * ATTN_WIDTH), BF16),
            jax.ShapeDtypeStruct((t, D_SSM), F32),
            jax.ShapeDtypeStruct((t, CONV_DIM), F32),
            jax.ShapeDtypeStruct((t, DT_PAD), F32),
        ],
        scratch_shapes=[pltpu.VMEM((tm, D_MODEL), BF16)],
        compiler_params=pltpu.CompilerParams(
            dimension_semantics=("parallel",), vmem_limit_bytes=VMEM_LIMIT),
        name="in_proj",
    )(x2d, n1, w_pad, gains, bd)


N_SLABS = 3 * ATTN_WIDTH // LANES
COPY_ROWS = 512
GATHER_UNROLL = 4
TWO_LEVEL_STRIDE = 4


def _dilated_kernel(qkv_ref, o_ref, slab_ref, tmp_ref, qr_ref, kr_ref, vt_ref, bias_ref, sa_ref, sb_ref,
                    st_ref, y_ref, lse_ref):
    nh = N_ATTN_HEADS
    w = ATTN_WIDTH
    seq = qkv_ref.shape[1]
    n_tiles = seq // BAND
    hm_b = _head_masks(w, BF16)

    @pl.when(pl.program_id(0) == 0)
    def _():
        c = lax.broadcasted_iota(jnp.int32, (2 * BAND, nh * BAND), 0)
        col = lax.broadcasted_iota(jnp.int32, (2 * BAND, nh * BAND), 1)
        a = col % BAND
        head = col // BAND
        slope = jnp.full(col.shape, SLOPES_A[-1], F32)
        for h in range(nh - 2, -1, -1):
            slope = jnp.where(head == h, SLOPES_A[h], slope)
        for p, (_, dil) in enumerate(DILATED_PATTERNS):
            for later, offset in enumerate((a - c, a + BAND - c)):
                valid = (offset >= 0) & (offset <= BAND)
                bias_ref[p, later] = jnp.where(valid, -(slope * (LOG2E * dil)) * offset.astype(F32), NEG_INF)

    def copy(i, carry):
        rows = pl.ds(pl.multiple_of(i * COPY_ROWS, COPY_ROWS), COPY_ROWS)
        x = qkv_ref[0, rows, :].astype(F32)
        for j in range(N_SLABS):
            slab_ref[j, rows, :] = x[:, j * LANES:(j + 1) * LANES]
        return carry
    lax.fori_loop(0, seq // COPY_ROWS, copy, 0)

    def run(p, dil, first_pattern):
        nblk = seq // dil // BAND

        def sub_rows(t):
            if dil == 1:
                return pl.ds(pl.multiple_of(t * BAND, BAND), BAND)
            return pl.ds(t // nblk + dil * BAND * (t % nblk), BAND, stride=dil)

        def gather(i, carry):
            for u in range(GATHER_UNROLL):
                t = i * GATHER_UNROLL + u
                parts = [slab_ref[j, sub_rows(t), :] for j in range(N_SLABS)]
                dst = pl.ds(pl.multiple_of(t * BAND, BAND), BAND)
                qr_ref[dst, :] = jnp.concatenate(parts[0:2], axis=1).astype(BF16)
                kr_ref[dst, :] = jnp.concatenate(parts[2:4], axis=1).astype(BF16)
                vt_ref[t] = _vt_with_ones(jnp.concatenate(parts[4:6], axis=1).T)
            return carry

        def gather_two_level(j):
            d1 = TWO_LEVEL_STRIDE
            d2 = dil // d1
            sec = seq // d1
            bps = n_tiles // d1
            lanes = slice((j % 2) * LANES, (j % 2 + 1) * LANES)
            ones_rows = (lax.broadcasted_iota(jnp.int32, (VT_ROWS - HEAD_DIM, BAND), 0) == 0).astype(BF16)

            def first(bb, carry):
                for r1 in range(d1):
                    src = pl.ds(r1 + d1 * BAND * bb, BAND, stride=d1)
                    dst = pl.ds(pl.multiple_of(r1 * sec + bb * BAND, BAND), BAND)
                    tmp_ref[dst, :] = slab_ref[j, src, :]
                return carry
            lax.fori_loop(0, bps, first, 0)

            def second(q, carry):
                for r1 in range(d1):
                    for m in range(nblk):
                        t = (d1 * q + r1) * nblk + m
                        src = pl.ds(r1 * sec + d2 * BAND * m + q, BAND, stride=d2)
                        part = tmp_ref[src, :]
                        dst = pl.ds(pl.multiple_of(t * BAND, BAND), BAND)
                        if j < 2:
                            qr_ref[dst, lanes] = part.astype(BF16)
                        elif j < 4:
                            kr_ref[dst, lanes] = part.astype(BF16)
                        else:
                            part_t = part.T
                            for hh in range(LANES // HEAD_DIM):
                                h0 = ((j % 2) * (LANES // HEAD_DIM) + hh) * VT_ROWS
                                vt_ref[t, h0:h0 + HEAD_DIM, :] = (
                                    part_t[hh * HEAD_DIM:(hh + 1) * HEAD_DIM, :].astype(BF16))
                                vt_ref[t, h0 + HEAD_DIM:h0 + VT_ROWS, :] = ones_rows
                return carry
            lax.fori_loop(0, d2, second, 0)

        if dil % (TWO_LEVEL_STRIDE * TWO_LEVEL_STRIDE) == 0:
            for j in range(N_SLABS):
                gather_two_level(j)
        else:
            lax.fori_loop(0, n_tiles // GATHER_UNROLL, gather, 0)

        def scores(t, dst_ref):
            tt = jnp.minimum(t, n_tiles - 1)
            later = jnp.minimum(tt % nblk, 1)
            q = qr_ref[pl.ds(pl.multiple_of(tt * BAND, BAND), BAND), :]
            q4 = jnp.concatenate([q * hm_b[h] for h in range(nh)], axis=0)
            k0 = pl.multiple_of((tt - later) * BAND, BAND)
            dst_ref[...] = _dot_nt(kr_ref[pl.ds(k0, 2 * BAND), :], q4) + bias_ref[p, later]

        n_slabs_out = w // LANES
        heads_per_slab = nh // n_slabs_out

        def softmax_pv(t, src_ref, slot, sl):
            kb = t - jnp.minimum(t % nblk, 1)
            for h in range(sl * heads_per_slab, (sl + 1) * heads_per_slab):
                cs = slice(h * BAND, (h + 1) * BAND)
                hr = slice(h * VT_ROWS, (h + 1) * VT_ROWS)
                fr = slice(h * HEAD_DIM, (h + 1) * HEAD_DIM)
                mx = jnp.max(src_ref[:, cs], axis=0, keepdims=True)
                pexp = jnp.exp2(src_ref[:, cs] - mx)
                vt = jnp.concatenate([vt_ref[kb, hr, :], vt_ref[kb + 1, hr, :]], axis=1)
                pv = _dot(vt, pexp.astype(BF16))
                l = pv[HEAD_DIM:HEAD_DIM + 1, :]
                st_ref[slot, 0, fr, :] = pv[0:HEAD_DIM, :] * (1.0 / l)
                st_ref[slot, 1, fr, :] = jnp.broadcast_to(mx + jnp.log2(l), (HEAD_DIM, BAND))

        def merge(m, slot, sl):
            fr = slice(sl * LANES, (sl + 1) * LANES)
            out = st_ref[slot, 0, fr, :].T
            lse = st_ref[slot, 1, fr, :].T
            rows = sub_rows(jnp.maximum(m, 0))
            if first_pattern:
                y_ref[sl, rows, :] = out
                lse_ref[sl, rows, :] = lse
            else:
                l_old = lse_ref[sl, rows, :]
                mx2 = jnp.maximum(l_old, lse)
                e_old = jnp.exp2(l_old - mx2)
                e_new = jnp.exp2(lse - mx2)
                den = e_old + e_new
                y_ref[sl, rows, :] = (y_ref[sl, rows, :] * e_old + out * e_new) / den
                lse_ref[sl, rows, :] = mx2 + jnp.log2(den)

        scores(0, sa_ref)
        st_ref[1, 0] = jnp.zeros(st_ref.shape[2:], F32)
        st_ref[1, 1] = jnp.full(st_ref.shape[2:], NEG_INF, F32)

        def pair(i, carry):
            t = 2 * i
            scores(t + 1, sb_ref)
            for sl in range(n_slabs_out):
                softmax_pv(t, sa_ref, 0, sl)
                merge(t - 1, 1, sl)
            scores(t + 2, sa_ref)
            for sl in range(n_slabs_out):
                softmax_pv(t + 1, sb_ref, 1, sl)
                merge(t, 0, sl)
            return carry
        lax.fori_loop(0, n_tiles // 2, pair, 0)
        for sl in range(n_slabs_out):
            merge(n_tiles - 1, 1, sl)

    order = sorted(range(len(DILATED_PATTERNS)), key=lambda p: -DILATED_PATTERNS[p][1])
    for p in order:
        run(p, DILATED_PATTERNS[p][1], p == order[0])

    def emit(i, carry):
        rows = pl.ds(pl.multiple_of(i * COPY_ROWS, COPY_ROWS), COPY_ROWS)
        o_ref[0, rows, :] = jnp.concatenate(
            [y_ref[sl, rows, :] for sl in range(w // LANES)], axis=1).astype(BF16)
        return carry
    lax.fori_loop(0, seq // COPY_ROWS, emit, 0)


def _dilated(qkv):
    b, s, _ = qkv.shape
    w = ATTN_WIDTH
    cols = N_ATTN_HEADS * BAND
    return pl.pallas_call(
        _dilated_kernel,
        grid=(b,),
        in_specs=[pl.BlockSpec((1, s, 3 * w), lambda bi: (bi, 0, 0))],
        out_specs=pl.BlockSpec((1, s, w), lambda bi: (bi, 0, 0)),
        out_shape=jax.ShapeDtypeStruct((b, s, w), BF16),
        scratch_shapes=[pltpu.VMEM((N_SLABS, s, LANES), F32),
                        pltpu.VMEM((s, LANES), F32),
                        pltpu.VMEM((s, w), BF16),
                        pltpu.VMEM((s, w), BF16),
                        pltpu.VMEM((s // BAND, N_ATTN_HEADS * VT_ROWS, BAND), BF16),
                        pltpu.VMEM((len(DILATED_PATTERNS), 2, 2 * BAND, cols), F32),
                        pltpu.VMEM((2 * BAND, cols), F32),
                        pltpu.VMEM((2 * BAND, cols), F32),
                        pltpu.VMEM((2, 2, w, BAND), F32),
                        pltpu.VMEM((w // LANES, s, LANES), F32),
                        pltpu.VMEM((w // LANES, s, LANES), F32)],
        compiler_params=pltpu.CompilerParams(
            dimension_semantics=("arbitrary",), vmem_limit_bytes=VMEM_LIMIT),
        name="dilated",
    )(qkv)


CONV_HALO = SUBLANES_F32


def _softplus(x):
    return jnp.maximum(x, 0.0) + jnp.log1p(jnp.exp(-jnp.abs(x)))


def _silu(x):
    return x / (1.0 + jnp.exp(-x))


def _ssd_kernel(xbc_ref, z_ref, dt_ref, cw_ref, cb_ref, dtb_ref, alog_ref, dsk_ref, nw_ref,
                y_ref, ext_ref, tail_ref, state_ref, *, chunks):
    rows = chunks * SSM_CHUNK
    halo = CONV_HALO
    cl = SSM_CHUNK
    gw = D_SSM // SSM_GROUPS
    hpg = SSM_HEADS // SSM_GROUPS

    @pl.when(pl.program_id(1) == 0)
    def _():
        tail_ref[...] = jnp.zeros_like(tail_ref)
        state_ref[...] = jnp.zeros_like(state_ref)

    ext_ref[0:halo, :] = tail_ref[...]
    ext_ref[halo:halo + rows, :] = xbc_ref[0]
    tail_ref[...] = xbc_ref[0, rows - halo:rows, :]
    conv = cb_ref[...] + cw_ref[SSM_CONV - 1:SSM_CONV, :] * ext_ref[halo:halo + rows, :]
    for j in range(1, SSM_CONV):
        conv = conv + cw_ref[SSM_CONV - 1 - j:SSM_CONV - j, :] * ext_ref[halo - j:halo - j + rows, :]
    xact = _silu(conv)

    dt = _softplus(dt_ref[0] + dtb_ref[...])
    da = dt * (-jnp.exp(alog_ref[...]))

    ri = lax.broadcasted_iota(jnp.int32, (cl, cl), 0)
    ci = lax.broadcasted_iota(jnp.int32, (cl, cl), 1)
    tri = ri >= ci
    tril = tri.astype(BF16)
    eh = lax.broadcasted_iota(jnp.int32, (DT_PAD, D_SSM), 0)
    el = lax.broadcasted_iota(jnp.int32, (DT_PAD, D_SSM), 1) // HEAD_DIM
    expand = (eh == el).astype(BF16)
    hm_g = _head_masks(gw, BF16)

    for c in range(chunks):
        r0 = c * cl
        xs = xact[r0:r0 + cl, 0:D_SSM]
        bm = xact[r0:r0 + cl, D_SSM:D_SSM + SSM_GROUPS * SSM_STATE]
        cm = xact[r0:r0 + cl, D_SSM + SSM_GROUPS * SSM_STATE:]
        acs = _dot_f32_rhs(tril, da[r0:r0 + cl])
        acs_t = acs.T
        acs_full = _dot_f32_lhs(acs, expand)
        dt_full = _dot_f32_lhs(dt[r0:r0 + cl], expand)
        xd = xs * dt_full
        last = acs_full[cl - 1:cl, :]
        xdd = xd * jnp.exp(last - acs_full)
        exp_acs = jnp.exp(acs_full)
        chunk_decay = jnp.exp(last)
        y_parts = []
        for g in range(SSM_GROUPS):
            gs = slice(g * gw, (g + 1) * gw)
            bg = bm[:, g * SSM_STATE:(g + 1) * SSM_STATE].astype(BF16)
            cg = cm[:, g * SSM_STATE:(g + 1) * SSM_STATE].astype(BF16)
            cb = _dot_nt(cg, bg)
            s_in = state_ref[:, gs]
            y_g = _dot(cg, s_in.astype(BF16)) * exp_acs[:, gs]
            state_ref[:, gs] = s_in * chunk_decay[:, gs] + _dot_tn(bg, xdd[:, gs].astype(BF16))
            xd_g = xd[:, gs].astype(BF16)
            for hh in range(hpg):
                h = g * hpg + hh
                diff = acs[:, h:h + 1] - acs_t[h:h + 1, :]
                lmat = jnp.exp(jnp.where(tri, diff, NEG_INF))
                y_g = y_g + _dot((cb * lmat).astype(BF16), xd_g * hm_g[hh])
            y_parts.append(y_g)
        y = jnp.concatenate(y_parts, axis=1) + dsk_ref[...] * xs
        y = y * _silu(z_ref[0, r0:r0 + cl, :])
        outs = []
        for g in range(SSM_GROUPS):
            yg = y[:, g * gw:(g + 1) * gw]
            ms = jnp.mean(yg * yg, axis=-1, keepdims=True)
            outs.append(yg * lax.rsqrt(ms + EPS) * nw_ref[:, g * gw:(g + 1) * gw])
        y_ref[0, r0:r0 + cl, :] = jnp.concatenate(outs, axis=1).astype(BF16)


def _ssd(xbc, z, dt, conv_w, conv_b, dt_bias, a_log, d_skip_full, norm_w, *, chunks=8):
    b, s, _ = xbc.shape
    rows = chunks * SSM_CHUNK
    blk = lambda width: pl.BlockSpec((1, rows, width), lambda bi, ci: (bi, ci, 0))
    const = lambda shape: pl.BlockSpec(shape, lambda bi, ci: (0, 0))
    return pl.pallas_call(
        functools.partial(_ssd_kernel, chunks=chunks),
        grid=(b, s // rows),
        in_specs=[blk(CONV_DIM), blk(D_SSM), blk(DT_PAD),
                  const((SSM_CONV, CONV_DIM)), const((1, CONV_DIM)), const((1, DT_PAD)),
                  const((1, DT_PAD)), const((1, D_SSM)), const((1, D_SSM))],
        out_specs=blk(D_SSM),
        out_shape=jax.ShapeDtypeStruct((b, s, D_SSM), BF16),
        scratch_shapes=[pltpu.VMEM((rows + CONV_HALO, CONV_DIM), F32),
                        pltpu.VMEM((CONV_HALO, CONV_DIM), F32),
                        pltpu.VMEM((SSM_STATE, D_SSM), F32)],
        compiler_params=pltpu.CompilerParams(
            dimension_semantics=("parallel", "arbitrary"), vmem_limit_bytes=VMEM_LIMIT),
        name="ssd",
    )(xbc, z, dt, conv_w, conv_b, dt_bias, a_log, d_skip_full, norm_w)


MOBA_QT = LANES


def _moba_kernel(q_ref, k_ref, v_ref, o_ref, kmean_ref, vt_ref, rel_ref, relown_ref, q4t_ref,
                 bias_ref, so_ref, sa_ref, sb_ref, mo_ref, ma_ref, mb_ref, acc_ref, m_ref, l_ref,
                 *, n_blk):
    nh = N_ATTN_HEADS
    bs = MOBA_BLOCK
    cols = nh * bs
    qb = pl.program_id(1)

    col = lax.broadcasted_iota(jnp.int32, (1, cols), 1)
    col_head = col // bs
    slope = jnp.full((1, cols), SLOPES_C[-1], F32)
    for h in range(nh - 2, -1, -1):
        slope = jnp.where(col_head == h, SLOPES_C[h], slope)

    @pl.when(qb == 0)
    def _():
        for n in range(n_blk):
            kblk = k_ref[0, n * bs:(n + 1) * bs, :].astype(F32)
            kmean_ref[n:n + 1, :] = jnp.sum(kblk, axis=0, keepdims=True) * (1.0 / bs)
            vt_ref[n] = _vt_with_ones(v_ref[0, n * bs:(n + 1) * bs, :].astype(F32).T)
        c = lax.broadcasted_iota(jnp.int32, (bs, cols), 0)
        a = lax.broadcasted_iota(jnp.int32, (bs, cols), 1) % bs
        rel = -(slope * LOG2E) * (a - c).astype(F32)
        rel_ref[...] = rel
        relown_ref[...] = jnp.where(a >= c, rel, NEG_INF)

    qt = q_ref[0].astype(F32).T
    frow = lax.broadcasted_iota(jnp.int32, (ATTN_WIDTH, bs), 0) // HEAD_DIM
    for h in range(nh):
        q4t_ref[:, h * bs:(h + 1) * bs] = jnp.where(frow == h, qt, 0.0).astype(BF16)

    def gate_rows():
        km = kmean_ref[...]
        km_hi = km.astype(BF16)
        km_lo = (km - km_hi.astype(F32)).astype(BF16)
        return jnp.concatenate([km_hi, km_lo], axis=0)

    def select_blocks(gate):
        blk =lax.broadcasted_iota(jnp.int32, (n_blk, cols), 0)
        blk_f = blk.astype(F32)
        past = blk < qb
        g = jnp.where(past, gate, NEG_INF)
        sel = jnp.zeros((n_blk, cols), jnp.bool_)
        for _ in range(MOBA_TOPK):
            mx = jnp.max(g, axis=0, keepdims=True)
            first = jnp.min(jnp.where(g == mx, blk_f, float(n_blk)), axis=0, keepdims=True)
            pick = blk_f == first
            sel = sel | pick
            g = jnp.where(pick, -jnp.inf, g)
        sel = sel & past
        bias_ref[...] = jnp.where(sel, -(slope * (LOG2E * bs)) * (qb - blk).astype(F32), NEG_INF)

    def scores(kb, buf, slot, rel_mat_ref, heads=range(N_ATTN_HEADS), extra_rows=None):
        dst_ref, mx_ref = buf
        kb = jnp.minimum(kb, n_blk - 1)
        lhs = k_ref[0, pl.ds(pl.multiple_of(kb * bs, bs), bs), :]
        if extra_rows is not None:
            lhs = jnp.concatenate([lhs, extra_rows], axis=0)
        extras = []
        for h in heads:
            hs = slice(h * bs, (h + 1) * bs)
            prod = _dot(lhs, q4t_ref[:, hs])
            s = prod[0:bs, :] + rel_mat_ref[:, hs]
            dst_ref[slot, :, hs] = s
            for i in range(bs // MOBA_QT):
                t = h * bs // MOBA_QT + i
                mx_ref[slot, t:t + 1, :] = jnp.max(s[:, i * MOBA_QT:(i + 1) * MOBA_QT], axis=0, keepdims=True)
            if extra_rows is not None:
                extras.append(prod[bs:, :])
        return jnp.concatenate(extras, axis=1) if extras else None

    def attend(kb, buf, bias_rows, slot=0, heads=range(N_ATTN_HEADS)):
        src_ref, mx_ref = buf
        first = bias_rows is None
        n = 1 if first else len(bias_rows)
        for h in heads:
            hs = slice(h * bs, (h + 1) * bs)
            ps, alphas = [[] for _ in range(n)], []
            tiles = range(h * bs // MOBA_QT, (h + 1) * bs // MOBA_QT)
            for t in tiles:
                cs = slice(t * MOBA_QT, (t + 1) * MOBA_QT)
                mxs = [mx_ref[slot + i, t:t + 1, :] for i in range(n)]
                if first:
                    m_new = mxs[0]
                    shifts = [m_new]
                else:
                    bs_t = [b[:, cs] for b in bias_rows]
                    m_old = m_ref[t:t + 1, :]
                    m_new = m_old
                    for mx, b in zip(mxs, bs_t):
                        m_new = jnp.maximum(m_new, mx + b)
                    shifts = [m_new - b for b in bs_t]
                    alphas.append(jnp.exp2(m_old - m_new))
                m_ref[t:t + 1, :] = m_new
                for i in range(n):
                    ps[i].append(jnp.exp2(src_ref[slot + i, :, cs] - shifts[i]).astype(BF16))
            p_all = jnp.concatenate([jnp.concatenate(p, axis=1) for p in ps], axis=0)
            vt = jnp.concatenate([vt_ref[kb + i, h * VT_ROWS:(h + 1) * VT_ROWS, :] for i in range(n)], axis=1)
            pv = _dot(vt, p_all)
            for i, t in enumerate(tiles):
                lsum = pv[HEAD_DIM:HEAD_DIM + 1, i * MOBA_QT:(i + 1) * MOBA_QT]
                l_ref[t:t + 1, :] = lsum if first else alphas[i] * l_ref[t:t + 1, :] + lsum
            pv = pv[0:HEAD_DIM, :]
            acc_ref[:, hs] = pv if first else jnp.concatenate(alphas, axis=1) * acc_ref[:, hs] + pv

    buf_own, buf_a, buf_b = (so_ref, mo_ref), (sa_ref, ma_ref), (sb_ref, mb_ref)
    gate2 = scores(qb, buf_own, 0, relown_ref, extra_rows=gate_rows())
    gate = gate2[0:n_blk, :] + gate2[n_blk:, :]
    scores(0, buf_a, 0, rel_ref)
    scores(1, buf_a, 1, rel_ref)
    attend(qb, buf_own, None)
    select_blocks(gate)

    def body(j, carry):
        kb = 2 * j

        def step(cur, nxt):
            for blk in range(2):
                bias_row = bias_ref[pl.ds(kb + blk, 1), :]
                for h in range(nh):
                    scores(kb + 2 + blk, nxt, blk, rel_ref, heads=(h,))
                    attend(kb + blk, cur, [bias_row], slot=blk, heads=(h,))

        @pl.when(j % 2 == 0)
        def _():
            step(buf_a, buf_b)

        @pl.when(j % 2 == 1)
        def _():
            step(buf_b, buf_a)

        return carry

    n_pairs = qb // 2
    lax.fori_loop(0, n_pairs, body, 0)

    for parity, cur in ((0, buf_a), (1, buf_b)):
        @pl.when((qb % 2 == 1) & (n_pairs % 2 == parity))
        def _(cur=cur):
            attend(qb - 1, cur, [bias_ref[pl.ds(qb - 1, 1), :]])

    inv = 1.0 / l_ref[...]
    heads = []
    for h in range(nh):
        tiles = range(h * bs // MOBA_QT, (h + 1) * bs // MOBA_QT)
        heads.append(jnp.concatenate(
            [acc_ref[:, t * MOBA_QT:(t + 1) * MOBA_QT] * inv[t:t + 1, :] for t in tiles], axis=1))
    o_ref[0] = jnp.concatenate(heads, axis=0).T.astype(BF16)


def _moba(qkv):
    b, s, _ = qkv.shape
    w = ATTN_WIDTH
    bs = MOBA_BLOCK
    n_blk = s // bs
    cols = N_ATTN_HEADS * bs
    return pl.pallas_call(
        functools.partial(_moba_kernel, n_blk=n_blk),
        grid=(b, n_blk),
        in_specs=[pl.BlockSpec((1, bs, w), lambda bi, qi: (bi, qi, 0)),
                  pl.BlockSpec((1, s, w), lambda bi, qi: (bi, 0, 1)),
                  pl.BlockSpec((1, s, w), lambda bi, qi: (bi, 0, 2))],
        out_specs=pl.BlockSpec((1, bs, w), lambda bi, qi: (bi, qi, 0)),
        out_shape=jax.ShapeDtypeStruct((b, s, w), BF16),
        scratch_shapes=[pltpu.VMEM((n_blk, w), F32),
                        pltpu.VMEM((n_blk, N_ATTN_HEADS * VT_ROWS, bs), BF16),
                        pltpu.VMEM((bs, cols), F32),
                        pltpu.VMEM((bs, cols), F32),
                        pltpu.VMEM((w, cols), BF16),
                        pltpu.VMEM((n_blk, cols), F32),
                        pltpu.VMEM((1, bs, cols), F32),
                        pltpu.VMEM((2, bs, cols), F32),
                        pltpu.VMEM((2, bs, cols), F32),
                        pltpu.VMEM((1, cols // MOBA_QT, MOBA_QT), F32),
                        pltpu.VMEM((2, cols // MOBA_QT, MOBA_QT), F32),
                        pltpu.VMEM((2, cols // MOBA_QT, MOBA_QT), F32),
                        pltpu.VMEM((HEAD_DIM, cols), F32),
                        pltpu.VMEM((cols // MOBA_QT, MOBA_QT), F32),
                        pltpu.VMEM((cols // MOBA_QT, MOBA_QT), F32)],
        compiler_params=pltpu.CompilerParams(
            dimension_semantics=("parallel", "arbitrary"), vmem_limit_bytes=VMEM_LIMIT),
        name="moba",
    )(qkv, qkv, qkv)


def _out_mlp_kernel(x_ref, ya_ref, yb_ref, yc_ref, wo_ref, n2_ref, w1_ref, w2_ref,
                    out_ref, h_ref, *, tf):
    tm = x_ref.shape[0]
    pr = tm // ROW_PARTS
    for s in range(ROW_PARTS):
        rows = slice(s * pr, (s + 1) * pr)
        y = jnp.concatenate([ya_ref[rows, :], yb_ref[rows, :], yc_ref[rows, :]], axis=1)
        x1 = x_ref[rows, :] + _dot(y, wo_ref[...])
        out_ref[rows, :] = x1
        ms = jnp.mean(x1 * x1, axis=-1, keepdims=True)
        h_ref[rows, :] = (x1 * lax.rsqrt(ms + EPS) * n2_ref[...]).astype(BF16)
    for c in range(D_FF // tf):
        cols = slice(c * tf, (c + 1) * tf)
        u = jnp.maximum(_dot(h_ref[...], w1_ref[:, cols]), 0.0)
        out_ref[...] += _dot((u * u).astype(BF16), w2_ref[cols, :])


def _out_mlp(x2d, ya, yb, yc, w_out, n2, w1, w2, *, tm=1024, tf=1024):
    t = x2d.shape[0]
    row = lambda width: pl.BlockSpec((tm, width), lambda i: (i, 0))
    resident = lambda shape: pl.BlockSpec(shape, lambda i: (0, 0), pipeline_mode=pl.Buffered(1))
    return pl.pallas_call(
        functools.partial(_out_mlp_kernel, tf=tf),
        grid=(t // tm,),
        in_specs=[row(D_MODEL), row(ATTN_WIDTH), row(D_SSM), row(ATTN_WIDTH),
                  resident((D_MODEL, D_MODEL)), resident((1, D_MODEL)),
                  resident((D_MODEL, D_FF)), resident((D_FF, D_MODEL))],
        out_specs=row(D_MODEL),
        out_shape=jax.ShapeDtypeStruct((t, D_MODEL), F32),
        scratch_shapes=[pltpu.VMEM((tm, D_MODEL), BF16)],
        compiler_params=pltpu.CompilerParams(
            dimension_semantics=("parallel",), vmem_limit_bytes=VMEM_LIMIT),
        name="out_mlp",
    )(x2d, ya, yb, yc, w_out, n2, w1, w2)


def _layer(x2d, bsz, seq, norm1_w, w_in, a_q_norm, a_k_norm, c_q_norm, c_k_norm, conv_w, conv_b,
           dt_bias, a_log, d_skip, ssm_norm_w, w_out, norm2_w, w_mlp_in, w_mlp_out):
    w_pad = jnp.pad(w_in, ((0, 0), (0, DT_PAD - SSM_HEADS))).astype(BF16)
    scale = HEAD_DIM ** -0.5 * LOG2E
    gains = jnp.stack([jnp.tile(a_q_norm, N_ATTN_HEADS) * scale, jnp.tile(a_k_norm, N_ATTN_HEADS),
                       jnp.tile(c_q_norm, N_ATTN_HEADS) * scale, jnp.tile(c_k_norm, N_ATTN_HEADS)])
    lane = jnp.arange(ATTN_WIDTH) // HEAD_DIM
    bd = (lane[:, None] == lane[None, :]).astype(BF16)
    pad_h = lambda v: jnp.pad(v, (0, DT_PAD - SSM_HEADS)).reshape(1, DT_PAD)

    qkva, qkvc, z, xbc, dt = _in_proj(x2d, norm1_w.reshape(1, -1), w_pad, gains, bd)

    ya = _dilated(qkva.reshape(bsz, seq, -1))
    yb = _ssd(xbc.reshape(bsz, seq, -1), z.reshape(bsz, seq, -1), dt.reshape(bsz, seq, -1),
              conv_w, conv_b.reshape(1, -1), pad_h(dt_bias), pad_h(a_log),
              jnp.repeat(d_skip, HEAD_DIM).reshape(1, -1), ssm_norm_w.reshape(1, -1))
    yc = _moba(qkvc.reshape(bsz, seq, -1))

    return _out_mlp(x2d, ya.reshape(bsz * seq, -1), yb.reshape(bsz * seq, -1), yc.reshape(bsz * seq, -1),
                    w_out.astype(BF16), norm2_w.reshape(1, -1),
                    w_mlp_in.astype(BF16), w_mlp_out.astype(BF16))


def kernel(x, norm1_w, w_in, a_q_norm, a_k_norm, c_q_norm, c_k_norm, conv_w, conv_b, dt_bias,
           a_log, d_skip, ssm_norm_w, w_out, norm2_w, w_mlp_in, w_mlp_out):
    bsz, seq, d = x.shape
    x2d = x.reshape(bsz * seq, d)
    for i in range(norm1_w.shape[0]):
        x2d = _layer(x2d, bsz, seq, norm1_w[i], w_in[i], a_q_norm[i], a_k_norm[i], c_q_norm[i],
                     c_k_norm[i], conv_w[i], conv_b[i], dt_bias[i], a_log[i], d_skip[i],
                     ssm_norm_w[i], w_out[i], norm2_w[i], w_mlp_in[i], w_mlp_out[i])
    return x2d.reshape(bsz, seq, d)
```
